```python
import numpy as np
import jax, jax.numpy as jnp
from jax import lax

D_MODEL = 1024
BATCH = 8
SEQ = 8192
DEPTH = 1
DEC_BATCH = 128
DEC_SEQ = 1
PAST_LEN = 8192
PAGE_SIZE = 128

GLA_HEADS = 8
GLA_DK = 32
GLA_DV = 64
GLA_RANK = 16
GLA_TAU = 16.0
GLA_CHUNK = 64
NSA_HEADS = 8
NSA_KV_HEADS = 2
NSA_REP = NSA_HEADS // NSA_KV_HEADS
NSA_DH = 64
L_CMP = 32
L_SEL = 64
N_SEL = 16
WINDOW = 512
Q_BLOCK = 128
D_FF = 2816
CONV_W = 3
EPS = 1e-6

GLA_WIDTH = GLA_HEADS * GLA_DV
NSA_WIDTH = NSA_HEADS * NSA_DH
MIX_WIDTH = GLA_WIDTH + NSA_WIDTH
KV_BRANCH = 2 * NSA_KV_HEADS * NSA_DH
SPLIT_SIZES = (GLA_HEADS * GLA_DK, GLA_HEADS * GLA_DK, GLA_WIDTH, GLA_RANK, GLA_WIDTH,
               NSA_WIDTH, KV_BRANCH, KV_BRANCH, KV_BRANCH, 3 * NSA_HEADS)
PROJ_WIDTH = sum(SPLIT_SIZES)

kernel_name = 'hymba_gla_nsa_convffn_step'


def rmsnorm(x, g):
    xf = x.astype(jnp.float32)
    y = xf * lax.rsqrt(jnp.mean(xf * xf, axis=-1, keepdims=True) + EPS)
    return (y * g.astype(jnp.float32)).astype(x.dtype)


def alibi_slopes():
    h = np.arange(1, NSA_HEADS + 1, dtype=np.float32)
    s = np.power(np.float32(2.0), -8.0 * h / NSA_HEADS).astype(np.float32)
    return jnp.asarray(s, dtype=jnp.float32).reshape(NSA_KV_HEADS, NSA_REP)


def masked_softmax(s, mask, axis=-1):
    s = jnp.where(mask, s, -1e30)
    p = jax.nn.softmax(s, axis=axis)
    return jnp.where(mask, p, 0.0)


def project(h, w_in, w_gla_a, b_gla_a, b_nsa_gate):
    f32 = jnp.float32
    B, T, _ = h.shape
    offs = np.cumsum(np.array(SPLIT_SIZES))[:-1].tolist()
    gq, gk, gv, ga, gr, nq, kvc, kvs, kvw, ng = jnp.split(h @ w_in, offs, axis=-1)
    gq = gq.reshape(B, T, GLA_HEADS, GLA_DK).astype(f32) * (GLA_DK ** -0.5)
    gk = gk.reshape(B, T, GLA_HEADS, GLA_DK).astype(f32)
    gv = gv.reshape(B, T, GLA_HEADS, GLA_DV).astype(f32)
    glog = (jax.nn.log_sigmoid((ga @ w_gla_a + b_gla_a).astype(f32)) / GLA_TAU).reshape(B, T, GLA_HEADS, GLA_DK)
    nq = nq.reshape(B, T, NSA_KV_HEADS, NSA_REP, NSA_DH).astype(f32) * (NSA_DH ** -0.5)
    kvc = kvc.reshape(B, T, 2, NSA_KV_HEADS, NSA_DH)
    kvs = kvs.reshape(B, T, 2, NSA_KV_HEADS, NSA_DH)
    kvw = kvw.reshape(B, T, 2, NSA_KV_HEADS, NSA_DH)
    gates = jax.nn.sigmoid((ng + b_nsa_gate).astype(f32)).reshape(B, T, NSA_KV_HEADS, NSA_REP, 3)
    return gq, gk, gv, glog, gr, nq, kvc, kvs, kvw, gates


def gla_mix(q, k, v, g, s0):
    B, T, H, _ = q.shape
    c = GLA_CHUNK if T % GLA_CHUNK == 0 else T
    n = T // c

    def chunks(a):
        return a.reshape(B, n, c, H, a.shape[-1]).transpose(1, 0, 3, 2, 4)

    causal = jnp.tril(jnp.ones((c, c), dtype=bool))[:, :, None]

    def step(S, xs):
        qc, kc, vc, gc = xs
        b = jnp.cumsum(gc, axis=2)
        o = jnp.einsum('bhid,bhde->bhie', qc * jnp.exp(b), S)
        decay = jnp.exp(jnp.where(causal, b[:, :, :, None, :] - b[:, :, None, :, :], -jnp.inf))
        a = jnp.einsum('bhid,bhjd,bhijd->bhij', qc, kc, decay)
        o = o + jnp.einsum('bhij,bhje->bhie', a, vc)
        b_last = b[:, :, -1:, :]
        S = jnp.exp(b_last[:, :, 0, :, None]) * S + jnp.einsum('bhjd,bhje->bhde', kc * jnp.exp(b_last - b), vc)
        return S, o

    S, o = lax.scan(step, s0, (chunks(q), chunks(k), chunks(v), chunks(g)))
    return o.transpose(1, 0, 3, 2, 4).reshape(B, T, H, -1), S


def gla_output(o, r, gain):
    o = o * lax.rsqrt(jnp.mean(o * o, axis=-1, keepdims=True) + EPS) * gain.astype(jnp.float32)
    B, T = o.shape[:2]
    return o.reshape(B, T, GLA_WIDTH) * jax.nn.silu(r.astype(jnp.float32))


def nsa_keys(kv_cmp, kv_sel):
    B, L = kv_cmp.shape[:2]
    f32 = jnp.float32
    cm = kv_cmp.astype(f32).reshape(B, L // L_CMP, L_CMP, 2, NSA_KV_HEADS, NSA_DH).mean(axis=2)
    ks_b = kv_sel[:, :, 0].reshape(B, L // L_SEL, L_SEL, NSA_KV_HEADS, NSA_DH).transpose(0, 3, 1, 2, 4)
    vs_b = kv_sel[:, :, 1].reshape(B, L // L_SEL, L_SEL, NSA_KV_HEADS, NSA_DH).transpose(0, 3, 1, 2, 4)
    return cm[:, :, 0], cm[:, :, 1], ks_b, vs_b


def nsa_block(q, q_pos, kc, vc, ks_b, vs_b, kw, vw, kw_pos, gates, slopes):
    f32 = jnp.float32
    B, Q = q.shape[:2]
    qp = q_pos.astype(f32)
    sl = slopes[None, None, :, :, None]
    nc = kc.shape[1]
    blk = jnp.arange(nc)
    c_end = (blk + 1) * L_CMP - 1
    c_mid = blk.astype(f32) * L_CMP + (L_CMP - 1) / 2.0
    dist_c = qp[:, None] - c_mid[None, :]
    s_c = jnp.einsum('bqgrd,bngd->bqgrn', q, kc) - sl * dist_c[None, :, None, None, :]
    mask_c = (c_end[None, :] <= q_pos[:, None])[None, :, None, None, :]
    p_c = masked_softmax(s_c, mask_c)
    o_c = jnp.einsum('bqgrn,bngd->bqgrd', p_c, vc)
    ns = ks_b.shape[2]
    imp = p_c.sum(axis=3).reshape(B, Q, NSA_KV_HEADS, ns, L_SEL // L_CMP).sum(axis=-1)
    sb = jnp.arange(ns)
    cur = (q_pos // L_SEL)[:, None]
    forced = (sb[None, :] == 0) | (sb[None, :] == cur) | (sb[None, :] == cur - 1)
    future = (sb * L_SEL)[None, :] > q_pos[:, None]
    imp = jnp.where(future[None, :, None, :], -jnp.inf, jnp.where(forced[None, :, None, :], jnp.inf, imp))
    _, idx = lax.top_k(imp, min(N_SEL, ns))
    idx_t = idx.transpose(0, 2, 1, 3)
    gather = jax.vmap(jax.vmap(lambda blocks, ix: blocks[ix]))
    k_sel = gather(ks_b, idx_t).astype(f32)
    v_sel = gather(vs_b, idx_t).astype(f32)
    tok = idx[..., None] * L_SEL + jnp.arange(L_SEL)
    dist_s = q_pos[None, :, None, None, None] - tok
    s_s = jnp.einsum('bqgrd,bgqkld->bqgrkl', q, k_sel) - slopes[None, None, :, :, None, None] * dist_s[:, :, :, None].astype(f32)
    p_s = masked_softmax(s_s, (dist_s >= 0)[:, :, :, None], axis=(-2, -1))
    o_s = jnp.einsum('bqgrkl,bgqkld->bqgrd', p_s, v_sel)
    dist_w = q_pos[:, None] - kw_pos[None, :]
    mask_w = ((dist_w >= 0) & (dist_w <= WINDOW) & (kw_pos >= 0)[None, :])[None, :, None, None, :]
    s_w = jnp.einsum('bqgrd,bkgd->bqgrk', q, kw.astype(f32)) - sl * dist_w.astype(f32)[None, :, None, None, :]
    p_w = masked_softmax(s_w, mask_w)
    o_w = jnp.einsum('bqgrk,bkgd->bqgrd', p_w, vw.astype(f32))
    o = gates[..., 0:1] * o_c + gates[..., 1:2] * o_s + gates[..., 2:3] * o_w
    return o.reshape(B, Q, NSA_WIDTH)


def nsa_prompt(q, kv_cmp, kv_sel, kv_win, gates, slopes):
    B, T = q.shape[:2]
    kc, vc, ks_b, vs_b = nsa_keys(kv_cmp, kv_sel)
    kw_pad = jnp.pad(kv_win, ((0, 0), (WINDOW, 0), (0, 0), (0, 0), (0, 0)))

    def one_block(i):
        q0 = i * Q_BLOCK
        qb = lax.dynamic_slice_in_dim(q, q0, Q_BLOCK, axis=1)
        gb = lax.dynamic_slice_in_dim(gates, q0, Q_BLOCK, axis=1)
        wb = lax.dynamic_slice_in_dim(kw_pad, q0, Q_BLOCK + WINDOW, axis=1)
        q_pos = q0 + jnp.arange(Q_BLOCK)
        kw_pos = q0 - WINDOW + jnp.arange(Q_BLOCK + WINDOW)
        return nsa_block(qb, q_pos, kc, vc, ks_b, vs_b, wb[:, :, 0], wb[:, :, 1], kw_pos, gb, slopes)

    out = lax.map(one_block, jnp.arange(T // Q_BLOCK))
    return out.transpose(1, 0, 2, 3).reshape(B, T, NSA_WIDTH)


def nsa_sample(q, full_cmp, full_sel, win, win_start, q_start, gates, slopes):
    L = full_cmp.shape[1]
    pad = (-L) % L_SEL
    padw = ((0, 0), (0, pad), (0, 0), (0, 0), (0, 0))
    kc, vc, ks_b, vs_b = nsa_keys(jnp.pad(full_cmp, padw), jnp.pad(full_sel, padw))
    q_pos = q_start + jnp.arange(q.shape[1])
    kw_pos = win_start + jnp.arange(win.shape[1])
    return nsa_block(q, q_pos, kc, vc, ks_b, vs_b, win[:, :, 0], win[:, :, 1], kw_pos, gates, slopes)


def merge(y_gla, y_nsa, w_out, dtype):
    return jnp.concatenate([y_gla, y_nsa], axis=-1).astype(dtype) @ w_out


def conv_ffn(h, prev, w_up, cw, cb, w_down):
    f32 = jnp.float32
    T = h.shape[1]
    a, b = jnp.split(h @ w_up, 2, axis=-1)
    a_ext = jnp.concatenate([prev.astype(a.dtype), a], axis=1)
    a_conv = cb.astype(f32)
    for j in range(CONV_W):
        a_conv = a_conv + cw[j].astype(f32) * a_ext[:, j:j + T].astype(f32)
    act = jax.nn.gelu(a_conv, approximate=False) * b.astype(f32)
    return act.astype(h.dtype) @ w_down, a_ext[:, -(CONV_W - 1):]


def setup_inputs(seed: int = 0) -> dict:
    key = jax.random.key(seed)
    ks = jax.random.split(key, 24)
    f32 = jnp.float32
    n_pages = PAST_LEN // PAGE_SIZE
    n_phys = (5 * DEC_BATCH * n_pages + 3) // 4
    w_buf = min(WINDOW, PAST_LEN)
    nrm = lambda k, s: jax.random.normal(k, s, dtype=f32)
    page_table = jax.random.permutation(ks[7], n_phys)[:DEC_BATCH * n_pages].reshape(DEC_BATCH, n_pages).astype(jnp.int32)
    return {
        'x_prompt': nrm(ks[0], (BATCH, SEQ, D_MODEL)),
        'x_sample': nrm(ks[1], (DEC_BATCH, DEC_SEQ, D_MODEL)),
        'cache_kv_cmp': nrm(ks[2], (DEPTH, n_phys, PAGE_SIZE, 2, NSA_KV_HEADS, NSA_DH)),
        'cache_kv_sel': nrm(ks[3], (DEPTH, n_phys, PAGE_SIZE, 2, NSA_KV_HEADS, NSA_DH)),
        'cache_kv_win': nrm(ks[4], (DEPTH, DEC_BATCH, w_buf, 2, NSA_KV_HEADS, NSA_DH)),
        'state_gla': nrm(ks[5], (DEPTH, DEC_BATCH, GLA_HEADS, GLA_DK, GLA_DV)),
        'state_conv': nrm(ks[6], (DEPTH, DEC_BATCH, CONV_W - 1, D_FF)),
        'page_table': page_table,
        'norm_mix': 1.0 + 0.01 * nrm(ks[8], (DEPTH, D_MODEL)),
        'w_in': nrm(ks[9], (DEPTH, D_MODEL, PROJ_WIDTH)) * D_MODEL ** -0.5,
        'w_gla_a': nrm(ks[10], (DEPTH, GLA_RANK, GLA_HEADS * GLA_DK)) * GLA_RANK ** -0.5,
        'b_gla_a': 0.01 * nrm(ks[11], (DEPTH, GLA_HEADS * GLA_DK)),
        'gla_norm': 1.0 + 0.01 * nrm(ks[12], (DEPTH, GLA_DV)),
        'b_nsa_gate': 0.01 * nrm(ks[13], (DEPTH, 3 * NSA_HEADS)),
        'w_out': nrm(ks[14], (DEPTH, MIX_WIDTH, D_MODEL)) * MIX_WIDTH ** -0.5,
        'norm_ffn': 1.0 + 0.01 * nrm(ks[15], (DEPTH, D_MODEL)),
        'w_ffn_up': nrm(ks[16], (DEPTH, D_MODEL, 2 * D_FF)) * D_MODEL ** -0.5,
        'conv_w': nrm(ks[17], (DEPTH, CONV_W, D_FF)) * CONV_W ** -0.5,
        'conv_b': 0.01 * nrm(ks[18], (DEPTH, D_FF)),
        'w_ffn_down': nrm(ks[19], (DEPTH, D_FF, D_MODEL)) * D_FF ** -0.5,
        'norm_final': 1.0 + 0.01 * nrm(ks[20], (D_MODEL,)),
    }


def reference(x_prompt, x_sample, cache_kv_cmp, cache_kv_sel, cache_kv_win, state_gla, state_conv, page_table,
              norm_mix, w_in, w_gla_a, b_gla_a, gla_norm, b_nsa_gate, w_out, norm_ffn, w_ffn_up, conv_w, conv_b,
              w_ffn_down, norm_final):
    f32 = jnp.float32
    slopes = alibi_slopes()
    bp, seq = x_prompt.shape[:2]
    db = x_sample.shape[0]
    past = page_table.shape[1] * cache_kv_cmp.shape[2]
    w_buf = cache_kv_win.shape[2]
    w_keep = min(WINDOW, seq)
    xp, xs = x_prompt, x_sample
    p_cmp, p_sel, p_win, p_gla, p_conv = [], [], [], [], []
    s_cmp, s_sel, s_win, s_gla, s_conv = [], [], [], [], []
    for l in range(DEPTH):
        gq, gk, gv, glog, gr, nq, kvc, kvs, kvw, gates = project(rmsnorm(xp, norm_mix[l]), w_in[l], w_gla_a[l], b_gla_a[l], b_nsa_gate[l])
        o_gla, S_p = gla_mix(gq, gk, gv, glog, jnp.zeros((bp, GLA_HEADS, GLA_DK, GLA_DV), f32))
        y_nsa = nsa_prompt(nq, kvc, kvs, kvw, gates, slopes)
        xp = xp + merge(gla_output(o_gla, gr, gla_norm[l]), y_nsa, w_out[l], xp.dtype)
        f, conv_p = conv_ffn(rmsnorm(xp, norm_ffn[l]), jnp.zeros((bp, CONV_W - 1, D_FF), xp.dtype), w_ffn_up[l], conv_w[l], conv_b[l], w_ffn_down[l])
        xp = xp + f
        p_cmp.append(kvc)
        p_sel.append(kvs)
        p_win.append(kvw[:, seq - w_keep:])
        p_gla.append(S_p)
        p_conv.append(conv_p)
        gq, gk, gv, glog, gr, nq, kvc, kvs, kvw, gates = project(rmsnorm(xs, norm_mix[l]), w_in[l], w_gla_a[l], b_gla_a[l], b_nsa_gate[l])
        o_gla, S_s = gla_mix(gq, gk, gv, glog, state_gla[l].astype(f32))
        past_c = cache_kv_cmp[l][page_table].reshape(db, past, 2, NSA_KV_HEADS, NSA_DH)
        past_s = cache_kv_sel[l][page_table].reshape(db, past, 2, NSA_KV_HEADS, NSA_DH)
        full_c = jnp.concatenate([past_c, kvc.astype(past_c.dtype)], axis=1)
        full_s = jnp.concatenate([past_s, kvs.astype(past_s.dtype)], axis=1)
        win = jnp.concatenate([cache_kv_win[l], kvw.astype(cache_kv_win.dtype)], axis=1)
        y_nsa = nsa_sample(nq, full_c, full_s, win, past - w_buf, past, gates, slopes)
        xs = xs + merge(gla_output(o_gla, gr, gla_norm[l]), y_nsa, w_out[l], xs.dtype)
        f, conv_s = conv_ffn(rmsnorm(xs, norm_ffn[l]), state_conv[l], w_ffn_up[l], conv_w[l], conv_b[l], w_ffn_down[l])
        xs = xs + f
        s_cmp.append(kvc)
        s_sel.append(kvs)
        s_win.append(win[:, win.shape[1] - w_buf:])
        s_gla.append(S_s)
        s_conv.append(conv_s)
    y_prompt = rmsnorm(xp, norm_final)
    y_sample = rmsnorm(xs, norm_final)
    return (y_prompt, y_sample,
            jnp.stack(p_cmp), jnp.stack(p_sel), jnp.stack(p_win), jnp.stack(p_gla), jnp.stack(p_conv),
            jnp.stack(s_cmp), jnp.stack(s_sel), jnp.stack(s_win), jnp.stack(s_gla), jnp.stack(s_conv))
```

```python
import functools

import numpy as np
import jax
import jax.numpy as jnp
from jax import lax
from jax.experimental import pallas as pl
from jax.experimental.pallas import tpu as pltpu

F32 = jnp.float32
BF16 = jnp.bfloat16

GLA_HEADS = 8
GLA_DK = 32
GLA_DV = 64
GLA_RANK = 16
GLA_TAU = 16.0
GLA_CHUNK = 64
NSA_HEADS = 8
NSA_G = 2
NSA_REP = 4
NSA_DH = 64
L_CMP = 32
L_SEL = 64
N_SEL = 16
WINDOW = 512
Q_BLOCK = 128
CONV_W = 3
EPS = 1e-6
NEG = -1e30
SLOPES = tuple(float(2.0 ** -(h + 1)) for h in range(NSA_HEADS))

GQ_W = GLA_HEADS * GLA_DK
GV_W = GLA_HEADS * GLA_DV
NQ_W = NSA_HEADS * NSA_DH
KV_W = 2 * NSA_G * NSA_DH
GATE_OFF = GLA_RANK
SEL_TILE = 256
VMEM_LIMIT = 56 * 1024 * 1024


def _nt(a, b):
    return lax.dot_general(a, b, (((1,), (1,)), ((), ())), preferred_element_type=F32)


def _tn(a, b):
    return lax.dot_general(a, b, (((0,), (0,)), ((), ())), preferred_element_type=F32)


def _dot(a, b):
    return jnp.dot(a, b, preferred_element_type=F32)


def _split3(x):
    hi = x.astype(BF16)
    r1 = x - hi.astype(F32)
    mid = r1.astype(BF16)
    lo = (r1 - mid.astype(F32)).astype(BF16)
    return hi, mid, lo


def _dot_exact_lhs(m_bf16, x):
    hi, mid, lo = _split3(x)
    return _dot(m_bf16, hi) + _dot(m_bf16, mid) + _dot(m_bf16, lo)


def _sigmoid(x):
    return 1.0 / (1.0 + jnp.exp(-x))


def _log_sigmoid(x):
    return jnp.minimum(x, 0.0) - jnp.log(1.0 + jnp.exp(-jnp.abs(x)))


def _rms(x, g):
    ms = jnp.mean(x * x, axis=-1, keepdims=True)
    return x * lax.rsqrt(ms + EPS) * g


def _const_spec(shape):
    nd = len(shape)
    return pl.BlockSpec(shape, lambda *_: (0,) * nd, pipeline_mode=pl.Buffered(1))


def _proj_kernel(x_ref, g_ref, wm_ref, ws_ref, wa_ref, ba_ref, bg_ref,
                 gq_ref, gk_ref, gv_ref, gr_ref, glog_ref, gates_ref,
                 kvc_ref, kvs_ref, kvw_ref, nqh_ref, *kv_head_refs):
    tm = x_ref.shape[0]
    h = _rms(x_ref[...], g_ref[...]).astype(BF16)

    def mm(a, b):
        return _dot(h, wm_ref[:, a:b])

    o = 0
    gq_ref[...] = mm(o, o + GQ_W) * (GLA_DK ** -0.5); o += GQ_W
    gk_ref[...] = mm(o, o + GQ_W); o += GQ_W
    gv_ref[...] = mm(o, o + GV_W); o += GV_W
    gr_ref[...] = mm(o, o + GV_W); o += GV_W
    nq = mm(o, o + NQ_W) * (NSA_DH ** -0.5); o += NQ_W
    kvc = mm(o, o + KV_W); o += KV_W
    kvs = mm(o, o + KV_W); o += KV_W
    kvw = mm(o, o + KV_W); o += KV_W
    kvc_ref[...] = kvc
    kvs_ref[...] = kvs
    kvw_ref[...] = kvw
    for hh in range(NSA_HEADS):
        nqh_ref[0, hh] = nq[:, hh * NSA_DH:(hh + 1) * NSA_DH].astype(BF16)
    if kv_head_refs:
        selh_ref, winh_ref, cmh_ref = kv_head_refs
        cm = kvc.reshape(tm // L_CMP, L_CMP, KV_W).sum(axis=1) * (1.0 / L_CMP)
        for j in range(2 * NSA_G):
            sl = slice(j * NSA_DH, (j + 1) * NSA_DH)
            selh_ref[0, j] = kvs[:, sl].astype(BF16)
            winh_ref[0, j] = kvw[:, sl].astype(BF16)
            cmh_ref[0, j] = cm[:, sl]
    small = _dot(h, ws_ref[...])
    z = _dot(small.astype(BF16), wa_ref[...]) + ba_ref[...]
    glog_ref[...] = _log_sigmoid(z) * (1.0 / GLA_TAU)
    gates_ref[...] = _sigmoid(small + bg_ref[...])


def _proj(x3, norm_g, wm, ws, wa, ba, bg, tm, kv_heads):
    b, t, d = x3.shape
    n = b * t
    nt = t // tm
    x2 = x3.reshape(n, d)
    row = lambda w: pl.BlockSpec((tm, w), lambda bi, ti: (bi * nt + ti, 0))
    hm = lambda nh, rows: pl.BlockSpec((1, nh, rows, NSA_DH), lambda bi, ti: (bi, 0, ti, 0))
    out_shape = [
        jax.ShapeDtypeStruct((n, GQ_W), F32), jax.ShapeDtypeStruct((n, GQ_W), F32),
        jax.ShapeDtypeStruct((n, GV_W), F32), jax.ShapeDtypeStruct((n, GV_W), F32),
        jax.ShapeDtypeStruct((n, GQ_W), F32), jax.ShapeDtypeStruct((n, 128), F32),
        jax.ShapeDtypeStruct((n, KV_W), F32), jax.ShapeDtypeStruct((n, KV_W), F32),
        jax.ShapeDtypeStruct((n, KV_W), F32),
        jax.ShapeDtypeStruct((b, NSA_HEADS, t, NSA_DH), BF16),
    ]
    out_specs = [row(GQ_W), row(GQ_W), row(GV_W), row(GV_W), row(GQ_W), row(128),
                 row(KV_W), row(KV_W), row(KV_W), hm(NSA_HEADS, tm)]
    if kv_heads:
        out_shape += [jax.ShapeDtypeStruct((b, 2 * NSA_G, t, NSA_DH), BF16),
                      jax.ShapeDtypeStruct((b, 2 * NSA_G, t, NSA_DH), BF16),
                      jax.ShapeDtypeStruct((b, 2 * NSA_G, t // L_CMP, NSA_DH), F32)]
        out_specs += [hm(2 * NSA_G, tm), hm(2 * NSA_G, tm), hm(2 * NSA_G, tm // L_CMP)]
    in_specs = [row(d), _const_spec((1, d)), _const_spec(wm.shape), _const_spec(ws.shape),
                _const_spec(wa.shape), _const_spec(ba.shape), _const_spec(bg.shape)]
    return pl.pallas_call(
        _proj_kernel, grid=(b, nt), in_specs=in_specs, out_specs=out_specs, out_shape=out_shape,
        compiler_params=pltpu.CompilerParams(dimension_semantics=("arbitrary", "arbitrary"),
                                             vmem_limit_bytes=VMEM_LIMIT),
        name="proj")(x2, norm_g, wm, ws, wa, ba, bg)


def _gla_consts():
    c = GLA_CHUNK
    t = np.arange(c)
    s = t[None, :]
    mats = [np.tril(np.ones((c, c), np.float32))]
    mqs, mks, masks = [], [], []
    m = c // 2
    while m >= 1:
        blk = t // (2 * m)
        mid = blk * 2 * m + m - 1
        upper = t > mid
        mqs.append((upper[:, None] & (s > mid[:, None]) & (s <= t[:, None])).astype(np.float32))
        mks.append(((~upper)[:, None] & (s > t[:, None]) & (s <= mid[:, None])).astype(np.float32))
        masks.append((upper[:, None] & (~upper)[None, :] & (blk[:, None] == blk[None, :])).astype(np.float32))
        m //= 2
    mcat = np.concatenate(mats + mqs + mks, axis=0)
    lvl = np.stack([np.tile(mk, (1, GLA_HEADS)) for mk in masks])
    eye = np.tile(np.eye(c, dtype=np.float32), (1, GLA_HEADS))
    hk = np.arange(GLA_HEADS * GLA_DK) // GLA_DK
    hc = np.arange(GLA_HEADS * c) // c
    hv = np.arange(GLA_HEADS * GLA_DV) // GLA_DV
    bdk = (hc[:, None] == hk[None, :]).astype(np.float32)
    bdv = (hc[:, None] == hv[None, :]).astype(np.float32)
    bds = (hv[:, None] == hk[None, :]).astype(np.float32)
    hsum = (hk[:, None] == hc[None, :]).astype(np.float32)
    hmean = (hv[:, None] == hv[None, :]).astype(np.float32) / GLA_DV
    return (jnp.asarray(mcat, BF16), jnp.asarray(lvl, F32), jnp.asarray(eye, F32), jnp.asarray(bdk, BF16),
            jnp.asarray(bdv, BF16), jnp.asarray(bds, F32), jnp.asarray(hsum, BF16), jnp.asarray(hmean, BF16))


def _gla_kernel(q_ref, k_ref, v_ref, g_ref, r_ref, gain_ref, mcat_ref, lvl_ref, eye_ref, bdk_ref, bdv_ref,
                bds_ref, hsum_ref, hmean_ref, y_ref, st_ref, s_sc):
    c = GLA_CHUNK
    nlev = lvl_ref.shape[0]
    n_chunks = q_ref.shape[0] // c

    @pl.when(pl.program_id(1) == 0)
    def _():
        s_sc[...] = jnp.zeros_like(s_sc)

    def chunk(ci, carry):
        r0 = pl.multiple_of(ci * c, c)
        q = q_ref[pl.ds(r0, c), :]
        k = k_ref[pl.ds(r0, c), :]
        v = v_ref[pl.ds(r0, c), :].astype(BF16)
        g = g_ref[pl.ds(r0, c), :]
        e_all = _dot_exact_lhs(mcat_ref[...], g)
        b = e_all[0:c]
        a = _dot((q * k).astype(BF16), hsum_ref[...]) * eye_ref[...]
        for lv in range(nlev):
            eq = e_all[(1 + lv) * c:(2 + lv) * c]
            ek = e_all[(1 + nlev + lv) * c:(2 + nlev + lv) * c]
            qt = (q * jnp.exp(eq)).astype(BF16)
            kt = (k * jnp.exp(ek)).astype(BF16)
            kbd = jnp.concatenate([kt] * GLA_HEADS, axis=0) * bdk_ref[...]
            a = a + _nt(qt, kbd) * lvl_ref[lv]
        vbd = jnp.concatenate([v] * GLA_HEADS, axis=0) * bdv_ref[...]
        s_t = s_sc[...]
        o = _dot(a.astype(BF16), vbd) + _nt((q * jnp.exp(b)).astype(BF16), s_t.astype(BF16))
        b_last = b[c - 1:c, :]
        kd = (k * jnp.exp(b_last - b)).astype(BF16)
        s_sc[...] = s_t * jnp.exp(b_last) + _tn(v, kd) * bds_ref[...]
        o2 = o * o
        o2h = o2.astype(BF16)
        o2l = (o2 - o2h.astype(F32)).astype(BF16)
        ms = _dot(o2h, hmean_ref[...]) + _dot(o2l, hmean_ref[...])
        r = r_ref[pl.ds(r0, c), :]
        y = o * lax.rsqrt(ms + EPS) * gain_ref[...] * (r * _sigmoid(r))
        y_ref[pl.ds(r0, c), :] = y.astype(BF16)
        return carry

    lax.fori_loop(0, n_chunks, chunk, 0)

    @pl.when(pl.program_id(1) == pl.num_programs(1) - 1)
    def _():
        st_ref[0] = s_sc[...]


def _gla_prompt(gq, gk, gv, glog, gr, gain, b, t, tc):
    n = b * t
    nt = t // tc
    consts = _gla_consts()
    gain_t = jnp.tile(gain.astype(F32).reshape(1, GLA_DV), (1, GLA_HEADS))
    row = lambda w: pl.BlockSpec((tc, w), lambda bi, ti: (bi * nt + ti, 0))
    in_specs = [row(GQ_W), row(GQ_W), row(GV_W), row(GQ_W), row(GV_W), _const_spec((1, GV_W))]
    in_specs += [_const_spec(a.shape) for a in consts]
    y, st = pl.pallas_call(
        _gla_kernel, grid=(b, nt), in_specs=in_specs,
        out_specs=[row(GV_W), pl.BlockSpec((1, GV_W, GQ_W), lambda bi, ti: (bi, 0, 0))],
        out_shape=[jax.ShapeDtypeStruct((n, GV_W), BF16), jax.ShapeDtypeStruct((b, GV_W, GQ_W), F32)],
        scratch_shapes=[pltpu.VMEM((GV_W, GQ_W), F32)],
        compiler_params=pltpu.CompilerParams(dimension_semantics=("arbitrary", "arbitrary"),
                                             vmem_limit_bytes=VMEM_LIMIT),
        name="gla_prompt")(gq, gk, gv, glog, gr, gain_t, *consts)
    s = jnp.stack([st[:, h * GLA_DV:(h + 1) * GLA_DV, h * GLA_DK:(h + 1) * GLA_DK] for h in range(GLA_HEADS)], axis=1)
    return y, s.transpose(0, 1, 3, 2)


def _gla_step_kernel(q_ref, k_ref, g_ref, v_ref, r_ref, gain_ref, s0_ref, y_ref, s1_ref):
    bb = q_ref.shape[0]
    hd = GLA_HEADS * GLA_DK
    ri = lax.broadcasted_iota(jnp.int32, (hd, hd), 0)
    ci = lax.broadcasted_iota(jnp.int32, (hd, hd), 1)
    eye = ri == ci
    rowhead = lax.broadcasted_iota(jnp.int32, (hd, GLA_DV), 0) // GLA_DK

    def col(x_row):
        return jnp.sum(jnp.where(eye, jnp.broadcast_to(x_row, (hd, hd)), 0.0), axis=1, keepdims=True)

    for i in range(bb):
        q = q_ref[i:i + 1, :]
        k = k_ref[i:i + 1, :]
        eg = jnp.exp(g_ref[i:i + 1, :])
        v3 = v_ref[i]
        s0 = s0_ref[i]
        vrows = jnp.zeros((hd, GLA_DV), F32)
        for h in range(GLA_HEADS):
            vrows = jnp.where(rowhead == h, jnp.broadcast_to(v3[h:h + 1, :], (hd, GLA_DV)), vrows)
        s1_ref[i] = col(eg) * s0 + col(k) * vrows
        t1 = (col(q * eg) * s0).reshape(GLA_HEADS, GLA_DK, GLA_DV).sum(axis=1)
        qk = col(q * k).reshape(GLA_HEADS, GLA_DK, 1).sum(axis=1)
        o = t1 + qk * v3
        ms = jnp.mean(o * o, axis=-1, keepdims=True)
        r3 = r_ref[i]
        y_ref[i] = (o * lax.rsqrt(ms + EPS) * gain_ref[...] * (r3 * _sigmoid(r3))).astype(BF16)


def _gla_step(gq, gk, glog, gv, gr, gain, s0, bb=8):
    n = gq.shape[0]
    hd = GLA_HEADS * GLA_DK
    v3 = gv.reshape(n, GLA_HEADS, GLA_DV)
    r3 = gr.reshape(n, GLA_HEADS, GLA_DV)
    s0r = s0.reshape(n, hd, GLA_DV)
    row = pl.BlockSpec((bb, hd), lambda i: (i, 0))
    h3 = pl.BlockSpec((bb, GLA_HEADS, GLA_DV), lambda i: (i, 0, 0))
    st = pl.BlockSpec((bb, hd, GLA_DV), lambda i: (i, 0, 0))
    y, s1 = pl.pallas_call(
        _gla_step_kernel, grid=(n // bb,),
        in_specs=[row, row, row, h3, h3, _const_spec((1, GLA_DV)), st],
        out_specs=[h3, st],
        out_shape=[jax.ShapeDtypeStruct((n, GLA_HEADS, GLA_DV), BF16), jax.ShapeDtypeStruct((n, hd, GLA_DV), F32)],
        compiler_params=pltpu.CompilerParams(dimension_semantics=("arbitrary",)),
        name="gla_step")(gq, gk, glog, v3, r3, gain.astype(F32).reshape(1, GLA_DV), s0r)
    return y.reshape(n, GV_W), s1


def _cmp_positions(nc):
    half = nc // 2
    col = lax.broadcasted_iota(jnp.int32, (1, nc), 1)
    return jnp.where(col < half, 2 * col, 2 * (col - half) + 1)


def _topk_mask(v, k_sel):
    n = v.shape[-1]
    sb = lax.broadcasted_iota(jnp.int32, v.shape, 1)
    sel = jnp.zeros(v.shape, F32)
    idxs = []
    for _ in range(k_sel):
        m = jnp.max(v, axis=-1, keepdims=True)
        idx = jnp.min(jnp.where(v == m, sb, n), axis=-1, keepdims=True)
        hit = sb == idx
        sel = jnp.where(hit, 1.0, sel)
        v = jnp.where(hit, -jnp.inf, v)
        idxs.append(idx)
    return sel, idxs


def _nsa_cmp_kernel(q_ref, cm_ref, oc_ref, selm_ref, any_ref):
    nc = cm_ref.shape[2]
    half = nc // 2
    ns = half
    q0 = pl.program_id(1) * Q_BLOCK
    qpos = q0 + lax.broadcasted_iota(jnp.int32, (Q_BLOCK, 1), 0)
    blk = _cmp_positions(nc)
    mask_c = ((blk + 1) * L_CMP - 1) <= qpos
    dist_c = qpos.astype(F32) - (blk.astype(F32) * L_CMP + (L_CMP - 1) / 2.0)
    sb = lax.broadcasted_iota(jnp.int32, (1, ns), 1)
    cur = qpos // L_SEL
    forced = (sb == 0) | (sb == cur) | (sb == cur - 1)
    future = sb * L_SEL > qpos
    for g in range(NSA_G):
        kc = cm_ref[0, g].astype(BF16)
        vc = cm_ref[0, NSA_G + g].astype(BF16)
        imp = jnp.zeros((Q_BLOCK, nc), F32)
        for r in range(NSA_REP):
            h = g * NSA_REP + r
            s = _nt(q_ref[0, h], kc) - SLOPES[h] * dist_c
            s = jnp.where(mask_c, s, NEG)
            e = jnp.exp(s - jnp.max(s, axis=-1, keepdims=True))
            p = e / jnp.sum(e, axis=-1, keepdims=True)
            p = jnp.where(mask_c, p, 0.0)
            oc_ref[0, h] = _dot(p.astype(BF16), vc)
            imp = imp + p
        imp2 = imp[:, :half] + imp[:, half:]
        v = jnp.where(future, -jnp.inf, jnp.where(forced, jnp.inf, imp2))
        sel, _ = _topk_mask(v, min(N_SEL, ns))
        selm_ref[0, g] = sel.astype(BF16)
        any_ref[0, g:g + 1, :] = jnp.max(sel, axis=0, keepdims=True)


def _nsa_cmp(nqh, cmh_perm):
    b, _, t, _ = nqh.shape
    nqb = t // Q_BLOCK
    nc = t // L_CMP
    ns = t // L_SEL
    hm = pl.BlockSpec((1, NSA_HEADS, Q_BLOCK, NSA_DH), lambda bi, i: (bi, 0, i, 0))
    return pl.pallas_call(
        _nsa_cmp_kernel, grid=(b, nqb),
        in_specs=[hm, pl.BlockSpec((1, 2 * NSA_G, nc, NSA_DH), lambda bi, i: (bi, 0, 0, 0))],
        out_specs=[hm, pl.BlockSpec((1, NSA_G, Q_BLOCK, ns), lambda bi, i: (bi, 0, i, 0)),
                   pl.BlockSpec((1, NSA_G, ns), lambda bi, i: (bi * nqb + i, 0, 0))],
        out_shape=[jax.ShapeDtypeStruct((b, NSA_HEADS, t, NSA_DH), F32),
                   jax.ShapeDtypeStruct((b, NSA_G, t, ns), BF16),
                   jax.ShapeDtypeStruct((b * nqb, NSA_G, ns), F32)],
        compiler_params=pltpu.CompilerParams(dimension_semantics=("arbitrary", "arbitrary"),
                                             vmem_limit_bytes=VMEM_LIMIT),
        name="nsa_cmp")(nqh, cmh_perm)


def _nsa_sel_kernel(bits_ref, q_ref, sel_ref, win_ref, selm_ref, oc_ref, gates_ref, y_ref, m_sc, l_sc, acc_sc):
    t_len = sel_ref.shape[2]
    ns = selm_ref.shape[3]
    nqb = pl.num_programs(1)
    bi = pl.program_id(0)
    i = pl.program_id(1)
    q0 = i * Q_BLOCK
    qpos = q0 + lax.broadcasted_iota(jnp.int32, (Q_BLOCK, 1), 0)
    n_tiles = (q0 + Q_BLOCK + SEL_TILE - 1) // SEL_TILE
    nwords = (t_len // SEL_TILE + 15) // 16
    win_keys = WINDOW + Q_BLOCK
    wstart = pl.multiple_of(jnp.clip(q0 - WINDOW, 0, t_len - win_keys), Q_BLOCK)
    wpos = wstart + lax.broadcasted_iota(jnp.int32, (1, win_keys), 1)
    wdist = qpos - wpos
    wvalid = (wdist >= 0) & (wdist <= WINDOW)
    wdist_f = wdist.astype(F32)
    gts = gates_ref[...]
    for g in range(NSA_G):
        m_sc[...] = jnp.full(m_sc.shape, NEG, F32)
        l_sc[...] = jnp.zeros_like(l_sc)
        acc_sc[...] = jnp.zeros_like(acc_sc)
        base = ((bi * nqb + i) * NSA_G + g) * nwords

        def tile(t, carry):
            word = bits_ref[base + t // 16]
            flag = lax.shift_right_logical(word, t % 16) & 1

            @pl.when(flag == 1)
            def _():
                ks = pl.multiple_of(t * SEL_TILE, SEL_TILE)
                kt = sel_ref[0, g, pl.ds(ks, SEL_TILE), :]
                vt = sel_ref[0, NSA_G + g, pl.ds(ks, SEL_TILE), :]
                kpos = ks + lax.broadcasted_iota(jnp.int32, (1, SEL_TILE), 1)
                expand = (lax.broadcasted_iota(jnp.int32, (ns, 1), 0) == kpos // L_SEL)
                mexp = _dot(selm_ref[0, g], jnp.where(expand, 1.0, 0.0).astype(BF16))
                valid = (mexp > 0.5) & (kpos <= qpos)
                dist = (qpos - kpos).astype(F32)
                for r in range(NSA_REP):
                    h = g * NSA_REP + r
                    s = _nt(q_ref[0, h], kt) - SLOPES[h] * dist
                    s = jnp.where(valid, s, NEG)
                    m_old = m_sc[r]
                    m_new = jnp.maximum(m_old, jnp.max(s, axis=-1, keepdims=True))
                    alpha = jnp.exp(m_old - m_new)
                    p = jnp.where(valid, jnp.exp(s - m_new), 0.0)
                    l_sc[r] = alpha * l_sc[r] + jnp.sum(p, axis=-1, keepdims=True)
                    acc_sc[r] = alpha * acc_sc[r] + _dot(p.astype(BF16), vt)
                    m_sc[r] = m_new

            return carry

        lax.fori_loop(0, n_tiles, tile, 0)
        kw = win_ref[0, g, pl.ds(wstart, win_keys), :]
        vw = win_ref[0, NSA_G + g, pl.ds(wstart, win_keys), :]
        for r in range(NSA_REP):
            h = g * NSA_REP + r
            o_s = acc_sc[r] / l_sc[r]
            s = _nt(q_ref[0, h], kw) - SLOPES[h] * wdist_f
            s = jnp.where(wvalid, s, NEG)
            e = jnp.exp(s - jnp.max(s, axis=-1, keepdims=True))
            p = jnp.where(wvalid, e / jnp.sum(e, axis=-1, keepdims=True), 0.0)
            o_w = _dot(p.astype(BF16), vw)
            c0 = GATE_OFF + h * 3
            o = gts[:, c0:c0 + 1] * oc_ref[0, h] + gts[:, c0 + 1:c0 + 2] * o_s + gts[:, c0 + 2:c0 + 3] * o_w
            y_ref[0, h] = o.astype(BF16)


def _nsa_sel(bits, nqh, selh, winh, selm, oc, gates):
    b, _, t, _ = nqh.shape
    nqb = t // Q_BLOCK
    ns = t // L_SEL
    hm = pl.BlockSpec((1, NSA_HEADS, Q_BLOCK, NSA_DH), lambda bi, i, *_: (bi, 0, i, 0))
    full = pl.BlockSpec((1, 2 * NSA_G, t, NSA_DH), lambda bi, i, *_: (bi, 0, 0, 0), pipeline_mode=pl.Buffered(1))
    grid_spec = pltpu.PrefetchScalarGridSpec(
        num_scalar_prefetch=1, grid=(b, nqb),
        in_specs=[hm, full, full,
                  pl.BlockSpec((1, NSA_G, Q_BLOCK, ns), lambda bi, i, *_: (bi, 0, i, 0)),
                  hm, pl.BlockSpec((Q_BLOCK, 128), lambda bi, i, *_: (bi * nqb + i, 0))],
        out_specs=[hm],
        scratch_shapes=[pltpu.VMEM((NSA_REP, Q_BLOCK, 1), F32), pltpu.VMEM((NSA_REP, Q_BLOCK, 1), F32),
                        pltpu.VMEM((NSA_REP, Q_BLOCK, NSA_DH), F32)])
    return pl.pallas_call(
        _nsa_sel_kernel, grid_spec=grid_spec,
        out_shape=[jax.ShapeDtypeStruct((b, NSA_HEADS, t, NSA_DH), BF16)],
        compiler_params=pltpu.CompilerParams(dimension_semantics=("arbitrary", "arbitrary"),
                                             vmem_limit_bytes=VMEM_LIMIT),
        name="nsa_sel")(bits, nqh, selh, winh, selm, oc, gates)[0]


def _tile_bits(anyf, ns):
    r = anyf.shape[0]
    per = SEL_TILE // L_SEL
    ntile = ns // per
    nwords = (ntile + 15) // 16
    tf = (anyf.reshape(r, NSA_G, ntile, per).max(axis=-1) > 0.5).astype(jnp.int32)
    tf = jnp.pad(tf, ((0, 0), (0, 0), (0, nwords * 16 - ntile))).reshape(r, NSA_G, nwords, 16)
    return jnp.sum(tf << jnp.arange(16, dtype=jnp.int32), axis=-1).reshape(-1).astype(jnp.int32)


PAGES_PER_STEP = 8


def _page_mean_kernel(pt_ref, *refs):
    o_ref = refs[-1]
    per = o_ref.shape[1] // PAGES_PER_STEP
    for p in range(PAGES_PER_STEP):
        x = refs[p][0]
        o_ref[0, p * per:(p + 1) * per, :] = x.reshape(per, L_CMP, x.shape[-1]).sum(axis=1) * (1.0 / L_CMP)


def _page_means(cache3, page_table):
    db, npg = page_table.shape
    _, psz, w = cache3.shape
    per = psz // L_CMP
    nj = npg // PAGES_PER_STEP

    def spec(p):
        return pl.BlockSpec((1, psz, w), lambda bi, j, pt: (pt[bi * npg + j * PAGES_PER_STEP + p], 0, 0))

    grid_spec = pltpu.PrefetchScalarGridSpec(
        num_scalar_prefetch=1, grid=(db, nj), in_specs=[spec(p) for p in range(PAGES_PER_STEP)],
        out_specs=[pl.BlockSpec((1, PAGES_PER_STEP * per, w), lambda bi, j, pt: (bi, j, 0))])
    return pl.pallas_call(
        _page_mean_kernel, grid_spec=grid_spec,
        out_shape=[jax.ShapeDtypeStruct((db, npg * per, w), F32)],
        compiler_params=pltpu.CompilerParams(dimension_semantics=("arbitrary", "arbitrary")),
        name="page_means")(page_table.reshape(-1), *([cache3] * PAGES_PER_STEP))[0]


def _slope_col():
    row = lax.broadcasted_iota(jnp.int32, (NSA_HEADS, 1), 0)
    sl = jnp.zeros((NSA_HEADS, 1), F32)
    for h in range(NSA_HEADS):
        sl = jnp.where(row == h, SLOPES[h], sl)
    return row, sl


def _nsa_cmp_step_kernel(q_ref, cm_ref, oc_ref, idx_ref, *, past, k_sel):
    nc = cm_ref.shape[1]
    half = nc // 2
    cm = cm_ref[0]
    blk = lax.broadcasted_iota(jnp.int32, (1, nc), 1)
    pair = jnp.where(lax.broadcasted_iota(jnp.int32, (nc, half), 0) // (L_SEL // L_CMP)
                     == lax.broadcasted_iota(jnp.int32, (nc, half), 1), 1.0, 0.0).astype(BF16)
    mask_c = ((blk + 1) * L_CMP - 1) <= past
    dist_c = float(past) - (blk.astype(F32) * L_CMP + (L_CMP - 1) / 2.0)
    row, sl = _slope_col()
    q8 = q_ref[0]
    sb = lax.broadcasted_iota(jnp.int32, (1, half), 1)
    forced = (sb == 0) | (sb == half - 1)
    lane = lax.broadcasted_iota(jnp.int32, (1, k_sel), 1)
    oc = jnp.zeros((NSA_HEADS, NSA_DH), F32)
    for g in range(NSA_G):
        kc = cm[:, g * NSA_DH:(g + 1) * NSA_DH].astype(BF16)
        vc = cm[:, (NSA_G + g) * NSA_DH:(NSA_G + g + 1) * NSA_DH].astype(BF16)
        s = _nt(q8, kc) - sl * dist_c
        s = jnp.where(mask_c, s, NEG)
        e = jnp.exp(s - jnp.max(s, axis=-1, keepdims=True))
        p = jnp.where(mask_c, e / jnp.sum(e, axis=-1, keepdims=True), 0.0)
        ingroup = (row // NSA_REP) == g
        oc = jnp.where(ingroup, _dot(p.astype(BF16), vc), oc)
        hi, mid, lo = _split3(jnp.where(ingroup, p, 0.0))
        imp2 = jnp.sum(_dot(hi, pair) + _dot(mid, pair) + _dot(lo, pair), axis=0, keepdims=True)
        _, idxs = _topk_mask(jnp.where(forced, jnp.inf, imp2), k_sel)
        acc = jnp.zeros((1, k_sel), jnp.int32)
        for t, ix in enumerate(idxs):
            acc = jnp.where(lane == t, ix, acc)
        idx_ref[0, g:g + 1, :] = acc
    oc_ref[0] = oc


def _nsa_cmp_step(q3, cm_s, past, k_sel):
    db = q3.shape[0]
    nc = cm_s.shape[1]
    return pl.pallas_call(
        functools.partial(_nsa_cmp_step_kernel, past=past, k_sel=k_sel), grid=(db,),
        in_specs=[pl.BlockSpec((1, NSA_HEADS, NSA_DH), lambda bi: (bi, 0, 0)),
                  pl.BlockSpec((1, nc, KV_W), lambda bi: (bi, 0, 0))],
        out_specs=[pl.BlockSpec((1, NSA_HEADS, NSA_DH), lambda bi: (bi, 0, 0)),
                   pl.BlockSpec((1, NSA_G, k_sel), lambda bi: (bi, 0, 0))],
        out_shape=[jax.ShapeDtypeStruct((db, NSA_HEADS, NSA_DH), F32),
                   jax.ShapeDtypeStruct((db, NSA_G, k_sel), jnp.int32)],
        compiler_params=pltpu.CompilerParams(dimension_semantics=("arbitrary",)),
        name="nsa_cmp_step")(q3, cm_s)


def _nsa_sel_step_kernel(pt_ref, ix_ref, q_ref, oc_ref, gates_ref, ks_new_ref, kw_new_ref, win_ref, cache_ref,
                         y_ref, wout_ref, blocks, sems, *, past, k_sel, npg, per_page):
    bi = pl.program_id(0)
    w_len = win_ref.shape[1]

    def block_copy(j):
        n = ix_ref[bi * (NSA_G * k_sel) + j]
        src = pt_ref[bi * npg + n // per_page] * per_page + n % per_page
        return pltpu.make_async_copy(cache_ref.at[src], blocks.at[j], sems.at[j])

    for j in range(NSA_G * k_sel):
        block_copy(j).start()
    row, sl = _slope_col()
    q8 = q_ref[0]
    q8f = q8.astype(F32)
    gts = gates_ref[0]
    gcol = []
    for c in range(3):
        col = jnp.zeros((NSA_HEADS, 1), F32)
        for h in range(NSA_HEADS):
            c0 = GATE_OFF + h * 3 + c
            col = jnp.where(row == h, gts[:, c0:c0 + 1], col)
        gcol.append(col)
    win = win_ref[0]
    ks_new = ks_new_ref[0]
    kw_new = kw_new_ref[0]
    lane = lax.broadcasted_iota(jnp.int32, (1, k_sel * L_SEL), 1)
    wpos = (past - w_len) + lax.broadcasted_iota(jnp.int32, (1, w_len), 1)
    wdist = past - wpos
    wvalid = (wdist <= WINDOW) & (wpos >= 0)

    def joint_attend(s, valid, k_new, v_new, vmat):
        s_new = jnp.sum(q8f * k_new.astype(BF16).astype(F32), axis=-1, keepdims=True)
        s = jnp.where(valid, s, NEG)
        m = jnp.maximum(jnp.max(s, axis=-1, keepdims=True), s_new)
        e = jnp.where(valid, jnp.exp(s - m), 0.0)
        en = jnp.exp(s_new - m)
        l = jnp.sum(e, axis=-1, keepdims=True) + en
        return _dot((e / l).astype(BF16), vmat) + (en / l).astype(BF16).astype(F32) * v_new.astype(BF16).astype(F32)

    ksl = [slice(g * NSA_DH, (g + 1) * NSA_DH) for g in range(NSA_G)]
    vsl = [slice((NSA_G + g) * NSA_DH, (NSA_G + g + 1) * NSA_DH) for g in range(NSA_G)]
    o_w = []
    for g in range(NSA_G):
        s = _nt(q8, win[:, ksl[g]].astype(BF16)) - sl * wdist.astype(F32)
        o_w.append(joint_attend(s, wvalid, kw_new[:, ksl[g]], kw_new[:, vsl[g]], win[:, vsl[g]].astype(BF16)))
    rows = lax.broadcasted_iota(jnp.int32, (w_len, 1), 0)
    wout_ref[0] = jnp.where(rows == w_len - 1, kw_new, pltpu.roll(win, w_len - 1, 0))
    for j in range(NSA_G * k_sel):
        block_copy(j).wait()
    o = jnp.zeros((NSA_HEADS, NSA_DH), F32)
    for g in range(NSA_G):
        kv = blocks[g * k_sel:(g + 1) * k_sel].reshape(k_sel * L_SEL, KV_W)
        kpos = lane % L_SEL
        for t in range(k_sel):
            kpos = kpos + jnp.where(lane // L_SEL == t, ix_ref[(bi * NSA_G + g) * k_sel + t] * L_SEL, 0)
        dist = past - kpos
        s = _nt(q8, kv[:, ksl[g]].astype(BF16)) - sl * dist.astype(F32)
        o_s = joint_attend(s, dist >= 0, ks_new[:, ksl[g]], ks_new[:, vsl[g]], kv[:, vsl[g]].astype(BF16))
        og = gcol[0] * oc_ref[0] + gcol[1] * o_s + gcol[2] * o_w[g]
        o = jnp.where((row // NSA_REP) == g, og, o)
    y_ref[0] = o.astype(BF16)


def _nsa_sel_step(page_table, idx, q3, oc, gates3, ks_new3, kw_new3, win3, cache_sel3, past, k_sel):
    db, npg = page_table.shape
    w_len = win3.shape[1]
    per_page = cache_sel3.shape[1] // L_SEL
    halves = cache_sel3.reshape(cache_sel3.shape[0] * per_page, L_SEL, KV_W)

    b3 = lambda w: pl.BlockSpec((1, 1, w), lambda bi, *_: (bi, 0, 0))
    h3 = pl.BlockSpec((1, NSA_HEADS, NSA_DH), lambda bi, *_: (bi, 0, 0))
    wsp = pl.BlockSpec((1, w_len, KV_W), lambda bi, *_: (bi, 0, 0))
    grid_spec = pltpu.PrefetchScalarGridSpec(
        num_scalar_prefetch=2, grid=(db,),
        in_specs=[h3, h3, b3(128), b3(KV_W), b3(KV_W), wsp, pl.BlockSpec(memory_space=pl.ANY)],
        out_specs=[h3, wsp],
        scratch_shapes=[pltpu.VMEM((NSA_G * k_sel, L_SEL, KV_W), F32), pltpu.SemaphoreType.DMA((NSA_G * k_sel,))])
    return pl.pallas_call(
        functools.partial(_nsa_sel_step_kernel, past=past, k_sel=k_sel, npg=npg, per_page=per_page),
        grid_spec=grid_spec,
        out_shape=[jax.ShapeDtypeStruct((db, NSA_HEADS, NSA_DH), BF16), jax.ShapeDtypeStruct((db, w_len, KV_W), F32)],
        compiler_params=pltpu.CompilerParams(dimension_semantics=("arbitrary",), vmem_limit_bytes=VMEM_LIMIT),
        name="nsa_sel_step")(page_table.reshape(-1), idx.reshape(-1), q3, oc, gates3, ks_new3, kw_new3, win3, halves)


FFN_CHUNK = 256


def _gelu(x):
    return 0.5 * x * (1.0 + lax.erf(x * (2.0 ** -0.5)))


def _ffn_kernel(*refs, seq_mode, tiles_per_seq):
    if seq_mode:
        (x_ref, yg_ref, yn_ref, wo_ref, nf_ref, wup_ref, cw_ref, cb_ref, wdn_ref, nl_ref,
         y_ref, conv_ref, abuf) = refs
    else:
        (x_ref, yg_ref, yn_ref, wo_ref, nf_ref, wup_ref, cw_ref, cb_ref, wdn_ref, nl_ref, prev_ref,
         y_ref, conv_ref) = refs
    tm = x_ref.shape[0]
    dff = wdn_ref.shape[0]
    gw = yg_ref.shape[1]
    x1 = x_ref[...] + _dot(yg_ref[...], wo_ref[0:gw, :]) + _dot(yn_ref[...], wo_ref[gw:, :])
    h2 = _rms(x1, nf_ref[...]).astype(BF16)
    if seq_mode:
        t_in = pl.program_id(0) % tiles_per_seq

        @pl.when(t_in == 0)
        def _():
            abuf[0:8, :] = jnp.zeros((8, dff), F32)

    acc = jnp.zeros(x1.shape, F32)
    for c0 in range(0, dff, FFN_CHUNK):
        cs = slice(c0, c0 + FFN_CHUNK)
        a = _dot(h2, wup_ref[:, cs])
        bgate = _dot(h2, wup_ref[:, dff + c0:dff + c0 + FFN_CHUNK])
        if seq_mode:
            abuf[8:8 + tm, cs] = a
            am1 = abuf[7:7 + tm, cs]
            am2 = abuf[6:6 + tm, cs]
        else:
            am2 = prev_ref[:, cs]
            am1 = prev_ref[:, dff + c0:dff + c0 + FFN_CHUNK]
            conv_ref[:, cs] = am1
            conv_ref[:, dff + c0:dff + c0 + FFN_CHUNK] = a
        conv = cb_ref[:, cs] + cw_ref[0:1, cs] * am2 + cw_ref[1:2, cs] * am1 + cw_ref[2:3, cs] * a
        act = (_gelu(conv) * bgate).astype(BF16)
        acc = acc + _dot(act, wdn_ref[cs, :])
    if seq_mode:
        @pl.when(t_in == tiles_per_seq - 1)
        def _():
            conv_ref[0] = abuf[8 + tm - (CONV_W - 1):8 + tm, :]

        abuf[0:8, :] = abuf[tm:tm + 8, :]
    y_ref[...] = _rms(x1 + acc, nl_ref[...])


def _ffn(x2, yg, yn, wo, nf, wup, cw, cb, wdn, nl, prev, b, t, tm):
    n, d = x2.shape
    dff = wdn.shape[0]
    seq_mode = prev is None
    row = lambda w: pl.BlockSpec((tm, w), lambda i: (i, 0))
    in_specs = [row(d), row(yg.shape[1]), row(yn.shape[1]), _const_spec(wo.shape), _const_spec((1, d)),
                _const_spec(wup.shape), _const_spec(cw.shape), _const_spec((1, dff)), _const_spec(wdn.shape),
                _const_spec((1, d))]
    args = [x2, yg, yn, wo, nf, wup, cw, cb, wdn, nl]
    if seq_mode:
        tps = t // tm
        conv_shape = jax.ShapeDtypeStruct((b, CONV_W - 1, dff), F32)
        conv_spec = pl.BlockSpec((1, CONV_W - 1, dff), lambda i: (i // tps, 0, 0))
        scratch = [pltpu.VMEM((8 + tm, dff), F32)]
    else:
        tps = 1
        in_specs.append(row(2 * dff))
        args.append(prev)
        conv_shape = jax.ShapeDtypeStruct((n, 2 * dff), F32)
        conv_spec = row(2 * dff)
        scratch = []
    return pl.pallas_call(
        functools.partial(_ffn_kernel, seq_mode=seq_mode, tiles_per_seq=tps), grid=(n // tm,),
        in_specs=in_specs, out_specs=[row(d), conv_spec],
        out_shape=[jax.ShapeDtypeStruct((n, d), F32), conv_shape], scratch_shapes=scratch,
        compiler_params=pltpu.CompilerParams(dimension_semantics=("arbitrary",), vmem_limit_bytes=VMEM_LIMIT),
        name="ffn_seq" if seq_mode else "ffn_step")(*args)


def _pack_in_weights(w_in, w_gla_a, b_gla_a, b_nsa_gate):
    sizes = (GQ_W, GQ_W, GV_W, GLA_RANK, GV_W, NQ_W, KV_W, KV_W, KV_W, 3 * NSA_HEADS)
    offs = np.concatenate([[0], np.cumsum(sizes)])
    part = lambda i: w_in[:, offs[i]:offs[i + 1]]
    wm = jnp.concatenate([part(0), part(1), part(2), part(4), part(5), part(6), part(7), part(8)], axis=1).astype(BF16)
    small = jnp.concatenate([part(3), part(9)], axis=1)
    ws = jnp.pad(small, ((0, 0), (0, 128 - small.shape[1]))).astype(BF16)
    wa = jnp.pad(w_gla_a, ((0, 128 - GLA_RANK), (0, 0))).astype(BF16)
    ba = b_gla_a.reshape(1, GQ_W).astype(F32)
    bg = jnp.pad(b_nsa_gate.reshape(1, -1), ((0, 0), (GATE_OFF, 128 - GATE_OFF - 3 * NSA_HEADS))).astype(F32)
    return wm, ws, wa, ba, bg


def kernel(x_prompt, x_sample, cache_kv_cmp, cache_kv_sel, cache_kv_win, state_gla, state_conv, page_table,
           norm_mix, w_in, w_gla_a, b_gla_a, gla_norm, b_nsa_gate, w_out, norm_ffn, w_ffn_up, conv_w, conv_b,
           w_ffn_down, norm_final):
    bp, seq, d = x_prompt.shape
    db, dseq, _ = x_sample.shape
    assert dseq == 1 and norm_mix.shape[0] == 1
    npg = page_table.shape[1]
    psz = cache_kv_cmp.shape[2]
    past = npg * psz
    n_phys = cache_kv_cmp.shape[1]
    w_buf = cache_kv_win.shape[2]
    dff = w_ffn_down.shape[1]
    assert past % L_SEL == 0 and w_buf == WINDOW and past >= WINDOW and seq >= WINDOW + Q_BLOCK
    tm = min(512, seq)

    wm, ws, wa, ba, bg = _pack_in_weights(w_in[0], w_gla_a[0], b_gla_a[0], b_nsa_gate[0])
    nmix = norm_mix[0].reshape(1, d).astype(F32)
    wo = w_out[0].astype(BF16)
    nf = norm_ffn[0].reshape(1, d).astype(F32)
    wup = w_ffn_up[0].astype(BF16)
    cw = conv_w[0].astype(F32)
    cb = conv_b[0].reshape(1, dff).astype(F32)
    wdn = w_ffn_down[0].astype(BF16)
    nl = norm_final.reshape(1, d).astype(F32)

    (gq, gk, gv, gr, glog, gates, kvc, kvs, kvw, nqh, selh, winh, cmh) = _proj(x_prompt, nmix, wm, ws, wa, ba, bg, tm, True)
    y_gla, s_p = _gla_prompt(gq, gk, gv, glog, gr, gla_norm[0], bp, seq, tm)
    nc = seq // L_CMP
    cmh_perm = jnp.concatenate([cmh[:, :, 0::2], cmh[:, :, 1::2]], axis=2)
    oc, selm, anyf = _nsa_cmp(nqh, cmh_perm)
    bits = _tile_bits(anyf, seq // L_SEL)
    ynh = _nsa_sel(bits, nqh, selh, winh, selm, oc, gates)
    y_nsa = ynh.transpose(0, 2, 1, 3).reshape(bp * seq, NQ_W)
    y_p, conv_p = _ffn(x_prompt.reshape(bp * seq, d), y_gla, y_nsa, wo, nf, wup, cw, cb, wdn, nl, None, bp, seq, tm)
    w_keep = min(WINDOW, seq)
    kv_shape = lambda n_b, n_t: (1, n_b, n_t, 2, NSA_G, NSA_DH)
    out_prompt = (y_p.reshape(bp, seq, d),
                  kvc.reshape(kv_shape(bp, seq)), kvs.reshape(kv_shape(bp, seq)),
                  kvw.reshape(bp, seq, KV_W)[:, seq - w_keep:].reshape(kv_shape(bp, w_keep)),
                  s_p[None], conv_p[None])

    xs3 = x_sample.reshape(1, db, d)
    (gq, gk, gv, gr, glog, gates, kvc_s, kvs_s, kvw_s, nqh) = _proj(xs3, nmix, wm, ws, wa, ba, bg, db, False)
    y_gla_s, s_s = _gla_step(gq, gk, glog, gv, gr, gla_norm[0], state_gla[0].astype(F32))
    q3 = nqh[0].transpose(1, 0, 2)
    cm_s = _page_means(cache_kv_cmp[0].reshape(n_phys, psz, KV_W), page_table)
    k_sel = min(N_SEL, past // L_SEL + 1) - 1
    oc_s, idx = _nsa_cmp_step(q3, cm_s, past, k_sel)
    yn_s, win_out = _nsa_sel_step(page_table, idx, q3, oc_s, gates.reshape(db, 1, 128), kvs_s.reshape(db, 1, KV_W),
                                  kvw_s.reshape(db, 1, KV_W), cache_kv_win[0].reshape(db, w_buf, KV_W),
                                  cache_kv_sel[0].reshape(n_phys, psz, KV_W), past, k_sel)
    y_s, conv_s = _ffn(x_sample.reshape(db, d), y_gla_s, yn_s.reshape(db, NQ_W), wo, nf, wup, cw, cb, wdn, nl,
                       state_conv[0].reshape(db, (CONV_W - 1) * dff).astype(F32), db, 1, db)
    out_sample = (y_s.reshape(db, 1, d),
                  kvc_s.reshape(kv_shape(db, 1)), kvs_s.reshape(kv_shape(db, 1)),
                  win_out.reshape(kv_shape(db, w_buf)),
                  s_s.reshape(1, db, GLA_HEADS, GLA_DK, GLA_DV), conv_s.reshape(1, db, CONV_W - 1, dff))
    return (out_prompt[0], out_sample[0]) + out_prompt[1:] + out_sample[1:]
```

```python
import functools

import numpy as np
import jax
import jax.numpy as jnp
from jax import lax
from jax.experimental import pallas as pl
from jax.experimental.pallas import tpu as pltpu

F32 = jnp.float32
BF16 = jnp.bfloat16

GLA_HEADS = 8
GLA_DK = 32
GLA_DV = 64
GLA_RANK = 16
GLA_TAU = 16.0
GLA_CHUNK = 64
NSA_HEADS = 8
NSA_G = 2
NSA_REP = 4
NSA_DH = 64
L_CMP = 32
L_SEL = 64
N_SEL = 16
WINDOW = 512
Q_BLOCK = 128
CONV_W = 3
EPS = 1e-6
NEG = -1e30
SLOPES = tuple(float(2.0 ** -(h + 1)) for h in range(NSA_HEADS))

GQ_W = GLA_HEADS * GLA_DK
GV_W = GLA_HEADS * GLA_DV
NQ_W = NSA_HEADS * NSA_DH
KV_W = 2 * NSA_G * NSA_DH
GATE_OFF = GLA_RANK
SEL_TILE = 256
VMEM_LIMIT = 56 * 1024 * 1024


def _nt(a, b):
    return lax.dot_general(a, b, (((1,), (1,)), ((), ())), preferred_element_type=F32)


def _tn(a, b):
    return lax.dot_general(a, b, (((0,), (0,)), ((), ())), preferred_element_type=F32)


def _dot(a, b):
    return jnp.dot(a, b, preferred_element_type=F32)


def _split3(x):
    hi = x.astype(BF16)
    r1 = x - hi.astype(F32)
    mid = r1.astype(BF16)
    lo = (r1 - mid.astype(F32)).astype(BF16)
    return hi, mid, lo


def _dot_exact_lhs(m_bf16, x):
    hi, mid, lo = _split3(x)
    return _dot(m_bf16, hi) + _dot(m_bf16, mid) + _dot(m_bf16, lo)


def _sigmoid(x):
    return 1.0 / (1.0 + jnp.exp(-x))


def _log_sigmoid(x):
    return jnp.minimum(x, 0.0) - jnp.log(1.0 + jnp.exp(-jnp.abs(x)))


def _rms(x, g):
    ms = jnp.mean(x * x, axis=-1, keepdims=True)
    return x * lax.rsqrt(ms + EPS) * g


def _const_spec(shape):
    nd = len(shape)
    return pl.BlockSpec(shape, lambda *_: (0,) * nd, pipeline_mode=pl.Buffered(1))


def _lane_halves(pair):
    return pair, pltpu.roll(pair, NSA_DH, 1)


def _proj_kernel(x_ref, g_ref, wm_ref, ws_ref, wa_ref, ba_ref, bg_ref,
                 gq_ref, gk_ref, gv_ref, gr_ref, glog_ref, kvc_ref, kvs_ref, kvw_ref, *attn_refs, prompt):
    tm = x_ref.shape[0]
    h = _rms(x_ref[...], g_ref[...]).astype(BF16)

    def mm(a, b):
        return _dot(h, wm_ref[:, a:b])

    o = 0
    gq_ref[...] = mm(o, o + GQ_W) * (GLA_DK ** -0.5); o += GQ_W
    gk_ref[...] = mm(o, o + GQ_W); o += GQ_W
    gv_ref[...] = mm(o, o + GV_W); o += GV_W
    gr_ref[...] = mm(o, o + GV_W); o += GV_W
    nq = mm(o, o + NQ_W) * (NSA_DH ** -0.5); o += NQ_W
    kvc = mm(o, o + KV_W); o += KV_W
    kvs = mm(o, o + KV_W); o += KV_W
    kvw = mm(o, o + KV_W); o += KV_W
    kvc_ref[...] = kvc
    kvs_ref[...] = kvs
    kvw_ref[...] = kvw
    small = _dot(h, ws_ref[...])
    z = _dot(small.astype(BF16), wa_ref[...]) + ba_ref[...]
    glog_ref[...] = _log_sigmoid(z) * (1.0 / GLA_TAU)
    gates = _sigmoid(small + bg_ref[...])
    if not prompt:
        nqh_ref, gates_ref = attn_refs
        for hh in range(NSA_HEADS):
            nqh_ref[0, hh] = nq[:, hh * NSA_DH:(hh + 1) * NSA_DH].astype(BF16)
        gates_ref[...] = gates
        return
    qa_ref, ksa_ref, kwa_ref, vst_ref, vwt_ref, cm_ref, gt_ref = attn_refs
    lane = lax.broadcasted_iota(jnp.int32, (tm, 128), 1)
    low = lane < NSA_DH
    for j in range(NSA_HEADS // 2):
        for hh, qh in zip((2 * j, 2 * j + 1), _lane_halves(nq[:, j * 128:(j + 1) * 128])):
            qa_ref[0, hh] = jnp.where(low, qh, jnp.where(lane == NSA_DH, SLOPES[hh], 0.0)).astype(BF16)
    pos = pl.program_id(1) * tm + lax.broadcasted_iota(jnp.int32, (tm, 128), 0)
    tile_pos = (pos % SEL_TILE).astype(F32)
    for src, ka_ref, vt_ref, tile in ((kvs, ksa_ref, vst_ref, SEL_TILE), (kvw, kwa_ref, vwt_ref, Q_BLOCK)):
        for g, kg in enumerate(_lane_halves(src[:, 0:NSA_G * NSA_DH])):
            ka_ref[0, g] = jnp.where(low, kg, tile_pos).astype(BF16)
        vt = src[:, NSA_G * NSA_DH:].T.astype(BF16)
        for g in range(NSA_G):
            for c in range(tm // tile):
                vt_ref[0, g, c] = vt[g * NSA_DH:(g + 1) * NSA_DH, c * tile:(c + 1) * tile]
    cm_ref[0] = kvc.reshape(tm // L_CMP, L_CMP, KV_W).sum(axis=1) * (1.0 / L_CMP)
    gt_ref[...] = gates.T


def _proj(x3, norm_g, wm, ws, wa, ba, bg, tm, prompt):
    b, t, d = x3.shape
    n = b * t
    nt = t // tm
    x2 = x3.reshape(n, d)
    row = lambda w: pl.BlockSpec((tm, w), lambda bi, ti: (bi * nt + ti, 0))
    sds = jax.ShapeDtypeStruct
    out_shape = [sds((n, GQ_W), F32), sds((n, GQ_W), F32), sds((n, GV_W), F32), sds((n, GV_W), F32),
                 sds((n, GQ_W), F32), sds((n, KV_W), F32), sds((n, KV_W), F32), sds((n, KV_W), F32)]
    out_specs = [row(GQ_W), row(GQ_W), row(GV_W), row(GV_W), row(GQ_W), row(KV_W), row(KV_W), row(KV_W)]
    if prompt:
        hm = lambda nh: pl.BlockSpec((1, nh, tm, 128), lambda bi, ti: (bi, 0, ti, 0))
        vt = lambda tile: pl.BlockSpec((1, NSA_G, tm // tile, NSA_DH, tile), lambda bi, ti: (bi, 0, ti, 0, 0))
        out_shape += [sds((b, NSA_HEADS, t, 128), BF16), sds((b, NSA_G, t, 128), BF16), sds((b, NSA_G, t, 128), BF16),
                      sds((b, NSA_G, t // SEL_TILE, NSA_DH, SEL_TILE), BF16),
                      sds((b, NSA_G, t // Q_BLOCK, NSA_DH, Q_BLOCK), BF16),
                      sds((b, t // L_CMP, KV_W), F32), sds((128, n), F32)]
        out_specs += [hm(NSA_HEADS), hm(NSA_G), hm(NSA_G), vt(SEL_TILE), vt(Q_BLOCK),
                      pl.BlockSpec((1, tm // L_CMP, KV_W), lambda bi, ti: (bi, ti, 0)),
                      pl.BlockSpec((128, tm), lambda bi, ti: (0, bi * nt + ti))]
    else:
        out_shape += [sds((b, NSA_HEADS, t, NSA_DH), BF16), sds((n, 128), F32)]
        out_specs += [pl.BlockSpec((1, NSA_HEADS, tm, NSA_DH), lambda bi, ti: (bi, 0, ti, 0)), row(128)]
    in_specs = [row(d), _const_spec((1, d)), _const_spec(wm.shape), _const_spec(ws.shape),
                _const_spec(wa.shape), _const_spec(ba.shape), _const_spec(bg.shape)]
    return pl.pallas_call(
        functools.partial(_proj_kernel, prompt=prompt), grid=(b, nt), in_specs=in_specs, out_specs=out_specs,
        out_shape=out_shape,
        compiler_params=pltpu.CompilerParams(dimension_semantics=("arbitrary", "arbitrary"),
                                             vmem_limit_bytes=VMEM_LIMIT),
        name="proj")(x2, norm_g, wm, ws, wa, ba, bg)


def _gla_consts():
    c = GLA_CHUNK
    t = np.arange(c)
    s = t[None, :]
    mats = [np.tril(np.ones((c, c), np.float32))]
    mqs, mks, masks = [], [], []
    m = c // 2
    while m >= 1:
        blk = t // (2 * m)
        mid = blk * 2 * m + m - 1
        upper = t > mid
        mqs.append((upper[:, None] & (s > mid[:, None]) & (s <= t[:, None])).astype(np.float32))
        mks.append(((~upper)[:, None] & (s > t[:, None]) & (s <= mid[:, None])).astype(np.float32))
        masks.append((upper[:, None] & (~upper)[None, :] & (blk[:, None] == blk[None, :])).astype(np.float32))
        m //= 2
    mcat = np.concatenate(mats + mqs + mks, axis=0)
    lvl = np.stack([np.tile(mk, (1, GLA_HEADS)) for mk in masks])
    eye = np.tile(np.eye(c, dtype=np.float32), (1, GLA_HEADS))
    hk = np.arange(GLA_HEADS * GLA_DK) // GLA_DK
    hc = np.arange(GLA_HEADS * c) // c
    hv = np.arange(GLA_HEADS * GLA_DV) // GLA_DV
    bdk = (hc[:, None] == hk[None, :]).astype(np.float32)
    bdv = (hc[:, None] == hv[None, :]).astype(np.float32)
    bds = (hv[:, None] == hk[None, :]).astype(np.float32)
    hsum = (hk[:, None] == hc[None, :]).astype(np.float32)
    hmean = (hv[:, None] == hv[None, :]).astype(np.float32) / GLA_DV
    return (jnp.asarray(mcat, BF16), jnp.asarray(lvl, F32), jnp.asarray(eye, F32), jnp.asarray(bdk, BF16),
            jnp.asarray(bdv, BF16), jnp.asarray(bds, F32), jnp.asarray(hsum, BF16), jnp.asarray(hmean, BF16))


def _gla_kernel(q_ref, k_ref, v_ref, g_ref, r_ref, gain_ref, mcat_ref, lvl_ref, eye_ref, bdk_ref, bdv_ref,
                bds_ref, hsum_ref, hmean_ref, y_ref, st_ref, s_sc):
    c = GLA_CHUNK
    nlev = lvl_ref.shape[0]
    n_chunks = q_ref.shape[0] // c

    @pl.when(pl.program_id(1) == 0)
    def _():
        s_sc[...] = jnp.zeros_like(s_sc)

    def chunk(ci, carry):
        r0 = pl.multiple_of(ci * c, c)
        q = q_ref[pl.ds(r0, c), :]
        k = k_ref[pl.ds(r0, c), :]
        v = v_ref[pl.ds(r0, c), :].astype(BF16)
        g = g_ref[pl.ds(r0, c), :]
        e_all = _dot_exact_lhs(mcat_ref[...], g)
        b = e_all[0:c]
        a = _dot((q * k).astype(BF16), hsum_ref[...]) * eye_ref[...]
        for lv in range(nlev):
            eq = e_all[(1 + lv) * c:(2 + lv) * c]
            ek = e_all[(1 + nlev + lv) * c:(2 + nlev + lv) * c]
            qt = (q * jnp.exp(eq)).astype(BF16)
            kt = (k * jnp.exp(ek)).astype(BF16)
            kbd = jnp.concatenate([kt] * GLA_HEADS, axis=0) * bdk_ref[...]
            a = a + _nt(qt, kbd) * lvl_ref[lv]
        vbd = jnp.concatenate([v] * GLA_HEADS, axis=0) * bdv_ref[...]
        s_t = s_sc[...]
        o = _dot(a.astype(BF16), vbd) + _nt((q * jnp.exp(b)).astype(BF16), s_t.astype(BF16))
        b_last = b[c - 1:c, :]
        kd = (k * jnp.exp(b_last - b)).astype(BF16)
        s_sc[...] = s_t * jnp.exp(b_last) + _tn(v, kd) * bds_ref[...]
        o2 = o * o
        o2h = o2.astype(BF16)
        o2l = (o2 - o2h.astype(F32)).astype(BF16)
        ms = _dot(o2h, hmean_ref[...]) + _dot(o2l, hmean_ref[...])
        r = r_ref[pl.ds(r0, c), :]
        y = o * lax.rsqrt(ms + EPS) * gain_ref[...] * (r * _sigmoid(r))
        y_ref[pl.ds(r0, c), :] = y.astype(BF16)
        return carry

    lax.fori_loop(0, n_chunks, chunk, 0)

    @pl.when(pl.program_id(1) == pl.num_programs(1) - 1)
    def _():
        st_ref[0] = s_sc[...]


def _gla_prompt(gq, gk, gv, glog, gr, gain, b, t, tc):
    n = b * t
    nt = t // tc
    consts = _gla_consts()
    gain_t = jnp.tile(gain.astype(F32).reshape(1, GLA_DV), (1, GLA_HEADS))
    row = lambda w: pl.BlockSpec((tc, w), lambda bi, ti: (bi * nt + ti, 0))
    in_specs = [row(GQ_W), row(GQ_W), row(GV_W), row(GQ_W), row(GV_W), _const_spec((1, GV_W))]
    in_specs += [_const_spec(a.shape) for a in consts]
    y, st = pl.pallas_call(
        _gla_kernel, grid=(b, nt), in_specs=in_specs,
        out_specs=[row(GV_W), pl.BlockSpec((1, GV_W, GQ_W), lambda bi, ti: (bi, 0, 0))],
        out_shape=[jax.ShapeDtypeStruct((n, GV_W), BF16), jax.ShapeDtypeStruct((b, GV_W, GQ_W), F32)],
        scratch_shapes=[pltpu.VMEM((GV_W, GQ_W), F32)],
        compiler_params=pltpu.CompilerParams(dimension_semantics=("arbitrary", "arbitrary"),
                                             vmem_limit_bytes=VMEM_LIMIT),
        name="gla_prompt")(gq, gk, gv, glog, gr, gain_t, *consts)
    s = jnp.stack([st[:, h * GLA_DV:(h + 1) * GLA_DV, h * GLA_DK:(h + 1) * GLA_DK] for h in range(GLA_HEADS)], axis=1)
    return y, s.transpose(0, 1, 3, 2)


def _gla_step_kernel(q_ref, k_ref, g_ref, v_ref, r_ref, gain_ref, s0_ref, y_ref, s1_ref):
    bb = q_ref.shape[0]
    hd = GLA_HEADS * GLA_DK
    ri = lax.broadcasted_iota(jnp.int32, (hd, hd), 0)
    ci = lax.broadcasted_iota(jnp.int32, (hd, hd), 1)
    eye = ri == ci
    rowhead = lax.broadcasted_iota(jnp.int32, (hd, GLA_DV), 0) // GLA_DK

    def col(x_row):
        return jnp.sum(jnp.where(eye, jnp.broadcast_to(x_row, (hd, hd)), 0.0), axis=1, keepdims=True)

    for i in range(bb):
        q = q_ref[i:i + 1, :]
        k = k_ref[i:i + 1, :]
        eg = jnp.exp(g_ref[i:i + 1, :])
        v3 = v_ref[i]
        s0 = s0_ref[i]
        vrows = jnp.zeros((hd, GLA_DV), F32)
        for h in range(GLA_HEADS):
            vrows = jnp.where(rowhead == h, jnp.broadcast_to(v3[h:h + 1, :], (hd, GLA_DV)), vrows)
        s1_ref[i] = col(eg) * s0 + col(k) * vrows
        t1 = (col(q * eg) * s0).reshape(GLA_HEADS, GLA_DK, GLA_DV).sum(axis=1)
        qk = col(q * k).reshape(GLA_HEADS, GLA_DK, 1).sum(axis=1)
        o = t1 + qk * v3
        ms = jnp.mean(o * o, axis=-1, keepdims=True)
        r3 = r_ref[i]
        y_ref[i] = (o * lax.rsqrt(ms + EPS) * gain_ref[...] * (r3 * _sigmoid(r3))).astype(BF16)


def _gla_step(gq, gk, glog, gv, gr, gain, s0, bb=8):
    n = gq.shape[0]
    hd = GLA_HEADS * GLA_DK
    v3 = gv.reshape(n, GLA_HEADS, GLA_DV)
    r3 = gr.reshape(n, GLA_HEADS, GLA_DV)
    s0r = s0.reshape(n, hd, GLA_DV)
    row = pl.BlockSpec((bb, hd), lambda i: (i, 0))
    h3 = pl.BlockSpec((bb, GLA_HEADS, GLA_DV), lambda i: (i, 0, 0))
    st = pl.BlockSpec((bb, hd, GLA_DV), lambda i: (i, 0, 0))
    y, s1 = pl.pallas_call(
        _gla_step_kernel, grid=(n // bb,),
        in_specs=[row, row, row, h3, h3, _const_spec((1, GLA_DV)), st],
        out_specs=[h3, st],
        out_shape=[jax.ShapeDtypeStruct((n, GLA_HEADS, GLA_DV), BF16), jax.ShapeDtypeStruct((n, hd, GLA_DV), F32)],
        compiler_params=pltpu.CompilerParams(dimension_semantics=("arbitrary",)),
        name="gla_step")(gq, gk, glog, v3, r3, gain.astype(F32).reshape(1, GLA_DV), s0r)
    return y.reshape(n, GV_W), s1


def _topk_mask_t(v, k_sel):
    n = v.shape[0]
    sb = lax.broadcasted_iota(jnp.int32, v.shape, 0)
    sel = jnp.zeros(v.shape, F32)
    for _ in range(k_sel):
        m = jnp.max(v, axis=0, keepdims=True)
        idx = jnp.min(jnp.where(v == m, sb, n), axis=0, keepdims=True)
        hit = sb == idx
        sel = jnp.where(hit, 1.0, sel)
        v = jnp.where(hit, -jnp.inf, v)
    return sel


def _topk_mask(v, k_sel):
    n = v.shape[-1]
    sb = lax.broadcasted_iota(jnp.int32, v.shape, 1)
    sel = jnp.zeros(v.shape, F32)
    idxs = []
    for _ in range(k_sel):
        m = jnp.max(v, axis=-1, keepdims=True)
        idx = jnp.min(jnp.where(v == m, sb, n), axis=-1, keepdims=True)
        hit = sb == idx
        sel = jnp.where(hit, 1.0, sel)
        v = jnp.where(hit, -jnp.inf, v)
        idxs.append(idx)
    return sel, idxs


def _group_queries(q_ref, g):
    return jnp.concatenate([q_ref[0, g * NSA_REP + r] for r in range(NSA_REP)], axis=0)


def _nsa_cmp_kernel(q_ref, cm_ref, oc_ref, selm_ref, any_ref):
    nc = cm_ref.shape[1]
    half = nc // 2
    ns = half
    q0 = pl.program_id(1) * Q_BLOCK
    qpos = q0 + lax.broadcasted_iota(jnp.int32, (1, Q_BLOCK), 1)
    row = lax.broadcasted_iota(jnp.int32, (nc, 1), 0)
    blk = jnp.where(row < half, 2 * row, 2 * (row - half) + 1)
    mask_c = ((blk + 1) * L_CMP - 1) <= qpos
    dist_c = qpos.astype(F32) - (blk.astype(F32) * L_CMP + (L_CMP - 1) / 2.0)
    sb = lax.broadcasted_iota(jnp.int32, (ns, 1), 0)
    cur = qpos // L_SEL
    forced = (sb == 0) | (sb == cur) | (sb == cur - 1)
    future = sb * L_SEL > qpos
    cm = cm_ref[0]
    low = lax.broadcasted_iota(jnp.int32, (nc, 128), 1) < NSA_DH
    vt_all = cm[:, NSA_G * NSA_DH:].T
    ones = jnp.ones((8, Q_BLOCK), BF16)
    for g, kg in enumerate(_lane_halves(cm[:, 0:NSA_G * NSA_DH])):
        kc = jnp.where(low, kg, 0.0).astype(BF16)
        vct = vt_all[g * NSA_DH:(g + 1) * NSA_DH, :].astype(BF16)
        s_all = _nt(kc, _group_queries(q_ref, g))
        imp = jnp.zeros((nc, Q_BLOCK), F32)
        ps = []
        for r in range(NSA_REP):
            h = g * NSA_REP + r
            s = s_all[:, r * Q_BLOCK:(r + 1) * Q_BLOCK] - SLOPES[h] * dist_c
            s = jnp.where(mask_c, s, NEG)
            e = jnp.exp(s - jnp.max(s, axis=0, keepdims=True))
            p = jnp.where(mask_c, e / jnp.sum(e, axis=0, keepdims=True), 0.0)
            ps.append(p.astype(BF16))
            imp = imp + p
        oct = _dot(vct, jnp.concatenate(ps, axis=1))
        for r in range(NSA_REP):
            oc_ref[0, g * NSA_REP + r] = oct[:, r * Q_BLOCK:(r + 1) * Q_BLOCK]
        imp2 = imp[:half] + imp[half:]
        v = jnp.where(future, -jnp.inf, jnp.where(forced, jnp.inf, imp2))
        sel = _topk_mask_t(v, min(N_SEL, ns))
        selm_ref[0, g] = sel
        any_ref[0, g:g + 1, :] = _nt(ones, sel.astype(BF16))[0:1]


def _nsa_cmp(qa, cm_perm):
    b, _, t, _ = qa.shape
    nqb = t // Q_BLOCK
    nc = t // L_CMP
    ns = t // L_SEL
    return pl.pallas_call(
        _nsa_cmp_kernel, grid=(b, nqb),
        in_specs=[pl.BlockSpec((1, NSA_HEADS, Q_BLOCK, 128), lambda bi, i: (bi, 0, i, 0)),
                  pl.BlockSpec((1, nc, KV_W), lambda bi, i: (bi, 0, 0))],
        out_specs=[pl.BlockSpec((1, NSA_HEADS, NSA_DH, Q_BLOCK), lambda bi, i: (bi, 0, 0, i)),
                   pl.BlockSpec((1, NSA_G, ns, Q_BLOCK), lambda bi, i: (bi, 0, 0, i)),
                   pl.BlockSpec((1, NSA_G, ns), lambda bi, i: (bi * nqb + i, 0, 0))],
        out_shape=[jax.ShapeDtypeStruct((b, NSA_HEADS, NSA_DH, t), F32),
                   jax.ShapeDtypeStruct((b, NSA_G, ns, t), F32),
                   jax.ShapeDtypeStruct((b * nqb, NSA_G, ns), F32)],
        compiler_params=pltpu.CompilerParams(dimension_semantics=("arbitrary", "arbitrary"),
                                             vmem_limit_bytes=VMEM_LIMIT),
        name="nsa_cmp")(qa, cm_perm)


WIN_TILES = (WINDOW + Q_BLOCK) // Q_BLOCK


def _nsa_sel_kernel(bits_ref, q_ref, ksa_ref, kwa_ref, vst_ref, vwt_ref, selm_ref, oc_ref, gt_ref, y_ref,
                    m_sc, l_sc, acc_sc):
    t_len = ksa_ref.shape[2]
    nqb = pl.num_programs(1)
    bi = pl.program_id(0)
    i = pl.program_id(1)
    q0 = i * Q_BLOCK
    qpos = q0 + lax.broadcasted_iota(jnp.int32, (1, Q_BLOCK), 1)
    diag = q0 // SEL_TILE
    nwords = (t_len // SEL_TILE + 15) // 16
    per = SEL_TILE // L_SEL
    wt0 = jnp.clip(i - WINDOW // Q_BLOCK, 0, t_len // Q_BLOCK - WIN_TILES)
    wstart = pl.multiple_of(wt0 * Q_BLOCK, Q_BLOCK)
    wkeys = WIN_TILES * Q_BLOCK
    wdist = qpos - (wstart + lax.broadcasted_iota(jnp.int32, (wkeys, 1), 0))
    wvalid = (wdist >= 0) & (wdist <= WINDOW)
    kiota = lax.broadcasted_iota(jnp.int32, (SEL_TILE, 1), 0)
    gts = gt_ref[...]
    for g in range(NSA_G):
        q_all = _group_queries(q_ref, g)
        m_sc[...] = jnp.full(m_sc.shape, NEG, F32)
        l_sc[...] = jnp.zeros_like(l_sc)
        acc_sc[...] = jnp.zeros_like(acc_sc)
        base = ((bi * nqb + i) * NSA_G + g) * nwords

        def attend_tile(t, causal):
            ks = pl.multiple_of(t * SEL_TILE, SEL_TILE)
            s_all = _nt(ksa_ref[0, g, pl.ds(ks, SEL_TILE), :], q_all)
            mk = jnp.concatenate([jnp.broadcast_to(selm_ref[0, g, pl.ds(t * per + j, 1), :], (L_SEL, Q_BLOCK))
                                  for j in range(per)], axis=0) > 0.5
            if causal:
                mk = mk & ((ks + kiota) <= qpos)
            shift = (ks - q0).astype(F32)
            s4 = jnp.concatenate(
                [jnp.where(mk, s_all[:, r * Q_BLOCK:(r + 1) * Q_BLOCK] + SLOPES[g * NSA_REP + r] * shift, NEG)
                 for r in range(NSA_REP)], axis=1)
            m_old = m_sc[...]
            m_new = jnp.maximum(m_old, jnp.max(s4, axis=0, keepdims=True))
            alpha = jnp.exp(m_old - m_new)
            p = jnp.exp(s4 - m_new)
            l_sc[...] = alpha * l_sc[...] + jnp.sum(p, axis=0, keepdims=True)
            acc_sc[...] = alpha * acc_sc[...] + _dot(vst_ref[0, g, t], p.astype(BF16))
            m_sc[...] = m_new

        def tile(t, carry):
            flag = lax.shift_right_logical(bits_ref[base + t // 16], t % 16) & 1

            @pl.when(flag == 1)
            def _():
                attend_tile(t, False)

            return carry

        lax.fori_loop(0, diag, tile, 0)
        attend_tile(diag, True)
        o_s = acc_sc[...] / l_sc[...]

        s_all = _nt(kwa_ref[0, g, pl.ds(wstart, wkeys), :], q_all)
        vw = jnp.concatenate([vwt_ref[0, g, wt0 + j] for j in range(WIN_TILES)], axis=1)
        shifts = [(((wstart + j * Q_BLOCK) // SEL_TILE) * SEL_TILE - q0).astype(F32) for j in range(WIN_TILES)]
        ps = []
        for r in range(NSA_REP):
            slope = SLOPES[g * NSA_REP + r]
            s = jnp.concatenate([s_all[j * Q_BLOCK:(j + 1) * Q_BLOCK, r * Q_BLOCK:(r + 1) * Q_BLOCK] + slope * shifts[j]
                                 for j in range(WIN_TILES)], axis=0)
            s = jnp.where(wvalid, s, NEG)
            e = jnp.exp(s - jnp.max(s, axis=0, keepdims=True))
            ps.append(jnp.where(wvalid, e / jnp.sum(e, axis=0, keepdims=True), 0.0).astype(BF16))
        o_w = _dot(vw, jnp.concatenate(ps, axis=1))

        outs = []
        for r in range(NSA_REP):
            h = g * NSA_REP + r
            c0 = GATE_OFF + h * 3
            cs = slice(r * Q_BLOCK, (r + 1) * Q_BLOCK)
            outs.append(gts[c0:c0 + 1, :] * oc_ref[0, h] + gts[c0 + 1:c0 + 2, :] * o_s[:, cs]
                        + gts[c0 + 2:c0 + 3, :] * o_w[:, cs])
        gw = NSA_REP * NSA_DH
        y_ref[:, g * gw:(g + 1) * gw] = jnp.concatenate(outs, axis=0).T.astype(BF16)


def _nsa_sel(bits, qa, ksa, kwa, vst, vwt, selm, oc, gates_t):
    b, _, t, _ = qa.shape
    nqb = t // Q_BLOCK
    ns = t // L_SEL
    whole = lambda a: pl.BlockSpec((1,) + a.shape[1:], lambda bi, i, *_: (bi,) + (0,) * (a.ndim - 1),
                                   pipeline_mode=pl.Buffered(1))
    cols = lambda lead: pl.BlockSpec((1,) + lead + (Q_BLOCK,), lambda bi, i, *_: (bi,) + (0,) * len(lead) + (i,))
    grid_spec = pltpu.PrefetchScalarGridSpec(
        num_scalar_prefetch=1, grid=(b, nqb),
        in_specs=[pl.BlockSpec((1, NSA_HEADS, Q_BLOCK, 128), lambda bi, i, *_: (bi, 0, i, 0)),
                  whole(ksa), whole(kwa), whole(vst), whole(vwt),
                  cols((NSA_G, ns)), cols((NSA_HEADS, NSA_DH)),
                  pl.BlockSpec((128, Q_BLOCK), lambda bi, i, *_: (0, bi * nqb + i))],
        out_specs=[pl.BlockSpec((Q_BLOCK, NQ_W), lambda bi, i, *_: (bi * nqb + i, 0))],
        scratch_shapes=[pltpu.VMEM((1, NSA_REP * Q_BLOCK), F32), pltpu.VMEM((1, NSA_REP * Q_BLOCK), F32),
                        pltpu.VMEM((NSA_DH, NSA_REP * Q_BLOCK), F32)])
    return pl.pallas_call(
        _nsa_sel_kernel, grid_spec=grid_spec,
        out_shape=[jax.ShapeDtypeStruct((b * t, NQ_W), BF16)],
        compiler_params=pltpu.CompilerParams(dimension_semantics=("arbitrary", "arbitrary"),
                                             vmem_limit_bytes=VMEM_LIMIT),
        name="nsa_sel")(bits, qa, ksa, kwa, vst, vwt, selm, oc, gates_t)[0]


def _tile_bits(anyf, ns):
    r = anyf.shape[0]
    per = SEL_TILE // L_SEL
    ntile = ns // per
    nwords = (ntile + 15) // 16
    tf = (anyf.reshape(r, NSA_G, ntile, per).max(axis=-1) > 0.5).astype(jnp.int32)
    tf = jnp.pad(tf, ((0, 0), (0, 0), (0, nwords * 16 - ntile))).reshape(r, NSA_G, nwords, 16)
    return jnp.sum(tf << jnp.arange(16, dtype=jnp.int32), axis=-1).reshape(-1).astype(jnp.int32)


PAGES_PER_STEP = 8


def _page_mean_kernel(pt_ref, *refs):
    o_ref = refs[-1]
    per = o_ref.shape[1] // PAGES_PER_STEP
    for p in range(PAGES_PER_STEP):
        x = refs[p][0]
        o_ref[0, p * per:(p + 1) * per, :] = x.reshape(per, L_CMP, x.shape[-1]).sum(axis=1) * (1.0 / L_CMP)


def _page_means(cache3, page_table):
    db, npg = page_table.shape
    _, psz, w = cache3.shape
    per = psz // L_CMP
    nj = npg // PAGES_PER_STEP

    def spec(p):
        return pl.BlockSpec((1, psz, w), lambda bi, j, pt: (pt[bi * npg + j * PAGES_PER_STEP + p], 0, 0))

    grid_spec = pltpu.PrefetchScalarGridSpec(
        num_scalar_prefetch=1, grid=(db, nj), in_specs=[spec(p) for p in range(PAGES_PER_STEP)],
        out_specs=[pl.BlockSpec((1, PAGES_PER_STEP * per, w), lambda bi, j, pt: (bi, j, 0))])
    return pl.pallas_call(
        _page_mean_kernel, grid_spec=grid_spec,
        out_shape=[jax.ShapeDtypeStruct((db, npg * per, w), F32)],
        compiler_params=pltpu.CompilerParams(dimension_semantics=("arbitrary", "arbitrary")),
        name="page_means")(page_table.reshape(-1), *([cache3] * PAGES_PER_STEP))[0]


def _slope_col():
    row = lax.broadcasted_iota(jnp.int32, (NSA_HEADS, 1), 0)
    sl = jnp.zeros((NSA_HEADS, 1), F32)
    for h in range(NSA_HEADS):
        sl = jnp.where(row == h, SLOPES[h], sl)
    return row, sl


def _nsa_cmp_step_kernel(q_ref, cm_ref, oc_ref, idx_ref, *, past, k_sel):
    nc = cm_ref.shape[1]
    half = nc // 2
    cm = cm_ref[0]
    blk = lax.broadcasted_iota(jnp.int32, (1, nc), 1)
    pair = jnp.where(lax.broadcasted_iota(jnp.int32, (nc, half), 0) // (L_SEL // L_CMP)
                     == lax.broadcasted_iota(jnp.int32, (nc, half), 1), 1.0, 0.0).astype(BF16)
    mask_c = ((blk + 1) * L_CMP - 1) <= past
    dist_c = float(past) - (blk.astype(F32) * L_CMP + (L_CMP - 1) / 2.0)
    row, sl = _slope_col()
    q8 = q_ref[0]
    sb = lax.broadcasted_iota(jnp.int32, (1, half), 1)
    forced = (sb == 0) | (sb == half - 1)
    lane = lax.broadcasted_iota(jnp.int32, (1, k_sel), 1)
    oc = jnp.zeros((NSA_HEADS, NSA_DH), F32)
    for g in range(NSA_G):
        kc = cm[:, g * NSA_DH:(g + 1) * NSA_DH].astype(BF16)
        vc = cm[:, (NSA_G + g) * NSA_DH:(NSA_G + g + 1) * NSA_DH].astype(BF16)
        s = _nt(q8, kc) - sl * dist_c
        s = jnp.where(mask_c, s, NEG)
        e = jnp.exp(s - jnp.max(s, axis=-1, keepdims=True))
        p = jnp.where(mask_c, e / jnp.sum(e, axis=-1, keepdims=True), 0.0)
        ingroup = (row // NSA_REP) == g
        oc = jnp.where(ingroup, _dot(p.astype(BF16), vc), oc)
        hi, mid, lo = _split3(jnp.where(ingroup, p, 0.0))
        imp2 = jnp.sum(_dot(hi, pair) + _dot(mid, pair) + _dot(lo, pair), axis=0, keepdims=True)
        _, idxs = _topk_mask(jnp.where(forced, jnp.inf, imp2), k_sel)
        acc = jnp.zeros((1, k_sel), jnp.int32)
        for t, ix in enumerate(idxs):
            acc = jnp.where(lane == t, ix, acc)
        idx_ref[0, g:g + 1, :] = acc
    oc_ref[0] = oc


def _nsa_cmp_step(q3, cm_s, past, k_sel):
    db = q3.shape[0]
    nc = cm_s.shape[1]
    return pl.pallas_call(
        functools.partial(_nsa_cmp_step_kernel, past=past, k_sel=k_sel), grid=(db,),
        in_specs=[pl.BlockSpec((1, NSA_HEADS, NSA_DH), lambda bi: (bi, 0, 0)),
                  pl.BlockSpec((1, nc, KV_W), lambda bi: (bi, 0, 0))],
        out_specs=[pl.BlockSpec((1, NSA_HEADS, NSA_DH), lambda bi: (bi, 0, 0)),
                   pl.BlockSpec((1, NSA_G, k_sel), lambda bi: (bi, 0, 0))],
        out_shape=[jax.ShapeDtypeStruct((db, NSA_HEADS, NSA_DH), F32),
                   jax.ShapeDtypeStruct((db, NSA_G, k_sel), jnp.int32)],
        compiler_params=pltpu.CompilerParams(dimension_semantics=("arbitrary",)),
        name="nsa_cmp_step")(q3, cm_s)


def _nsa_sel_step_kernel(pt_ref, ix_ref, q_ref, oc_ref, gates_ref, ks_new_ref, kw_new_ref, win_ref, cache_ref,
                         y_ref, wout_ref, blocks, sems, *, past, k_sel, npg, per_page):
    bi = pl.program_id(0)
    w_len = win_ref.shape[1]

    def block_copy(j):
        n = ix_ref[bi * (NSA_G * k_sel) + j]
        page = pt_ref[bi * npg + n // per_page]
        rows = pl.ds(pl.multiple_of((n % per_page) * L_SEL, L_SEL), L_SEL)
        return pltpu.make_async_copy(cache_ref.at[page, rows], blocks.at[j], sems.at[j])

    for j in range(NSA_G * k_sel):
        block_copy(j).start()
    row, sl = _slope_col()
    q8 = q_ref[0]
    q8f = q8.astype(F32)
    gts = gates_ref[0]
    gcol = []
    for c in range(3):
        col = jnp.zeros((NSA_HEADS, 1), F32)
        for h in range(NSA_HEADS):
            c0 = GATE_OFF + h * 3 + c
            col = jnp.where(row == h, gts[:, c0:c0 + 1], col)
        gcol.append(col)
    win = win_ref[0]
    ks_new = ks_new_ref[0]
    kw_new = kw_new_ref[0]
    lane = lax.broadcasted_iota(jnp.int32, (1, k_sel * L_SEL), 1)
    wpos = (past - w_len) + lax.broadcasted_iota(jnp.int32, (1, w_len), 1)
    wdist = past - wpos
    wvalid = (wdist <= WINDOW) & (wpos >= 0)

    def joint_attend(s, valid, k_new, v_new, vmat):
        s_new = jnp.sum(q8f * k_new.astype(BF16).astype(F32), axis=-1, keepdims=True)
        s = jnp.where(valid, s, NEG)
        m = jnp.maximum(jnp.max(s, axis=-1, keepdims=True), s_new)
        e = jnp.where(valid, jnp.exp(s - m), 0.0)
        en = jnp.exp(s_new - m)
        l = jnp.sum(e, axis=-1, keepdims=True) + en
        return _dot((e / l).astype(BF16), vmat) + (en / l).astype(BF16).astype(F32) * v_new.astype(BF16).astype(F32)

    ksl = [slice(g * NSA_DH, (g + 1) * NSA_DH) for g in range(NSA_G)]
    vsl = [slice((NSA_G + g) * NSA_DH, (NSA_G + g + 1) * NSA_DH) for g in range(NSA_G)]
    o_w = []
    for g in range(NSA_G):
        s = _nt(q8, win[:, ksl[g]].astype(BF16)) - sl * wdist.astype(F32)
        o_w.append(joint_attend(s, wvalid, kw_new[:, ksl[g]], kw_new[:, vsl[g]], win[:, vsl[g]].astype(BF16)))
    rows = lax.broadcasted_iota(jnp.int32, (w_len, 1), 0)
    wout_ref[0] = jnp.where(rows == w_len - 1, kw_new, pltpu.roll(win, w_len - 1, 0))
    for j in range(NSA_G * k_sel):
        block_copy(j).wait()
    o = jnp.zeros((NSA_HEADS, NSA_DH), F32)
    for g in range(NSA_G):
        kv = blocks[g * k_sel:(g + 1) * k_sel].reshape(k_sel * L_SEL, KV_W)
        kpos = lane % L_SEL
        for t in range(k_sel):
            kpos = kpos + jnp.where(lane // L_SEL == t, ix_ref[(bi * NSA_G + g) * k_sel + t] * L_SEL, 0)
        dist = past - kpos
        s = _nt(q8, kv[:, ksl[g]].astype(BF16)) - sl * dist.astype(F32)
        o_s = joint_attend(s, dist >= 0, ks_new[:, ksl[g]], ks_new[:, vsl[g]], kv[:, vsl[g]].astype(BF16))
        og = gcol[0] * oc_ref[0] + gcol[1] * o_s + gcol[2] * o_w[g]
        o = jnp.where((row // NSA_REP) == g, og, o)
    y_ref[0] = o.astype(BF16)


def _nsa_sel_step(page_table, idx, q3, oc, gates3, ks_new3, kw_new3, win3, cache_sel3, past, k_sel):
    db, npg = page_table.shape
    w_len = win3.shape[1]
    per_page = cache_sel3.shape[1] // L_SEL

    b3 = lambda w: pl.BlockSpec((1, 1, w), lambda bi, *_: (bi, 0, 0))
    h3 = pl.BlockSpec((1, NSA_HEADS, NSA_DH), lambda bi, *_: (bi, 0, 0))
    wsp = pl.BlockSpec((1, w_len, KV_W), lambda bi, *_: (bi, 0, 0))
    grid_spec = pltpu.PrefetchScalarGridSpec(
        num_scalar_prefetch=2, grid=(db,),
        in_specs=[h3, h3, b3(128), b3(KV_W), b3(KV_W), wsp, pl.BlockSpec(memory_space=pl.ANY)],
        out_specs=[h3, wsp],
        scratch_shapes=[pltpu.VMEM((NSA_G * k_sel, L_SEL, KV_W), F32), pltpu.SemaphoreType.DMA((NSA_G * k_sel,))])
    return pl.pallas_call(
        functools.partial(_nsa_sel_step_kernel, past=past, k_sel=k_sel, npg=npg, per_page=per_page),
        grid_spec=grid_spec,
        out_shape=[jax.ShapeDtypeStruct((db, NSA_HEADS, NSA_DH), BF16), jax.ShapeDtypeStruct((db, w_len, KV_W), F32)],
        compiler_params=pltpu.CompilerParams(dimension_semantics=("arbitrary",), vmem_limit_bytes=VMEM_LIMIT),
        name="nsa_sel_step")(page_table.reshape(-1), idx.reshape(-1), q3, oc, gates3, ks_new3, kw_new3, win3, cache_sel3)


FFN_CHUNK = 256


def _gelu(x):
    return 0.5 * x * (1.0 + lax.erf(x * (2.0 ** -0.5)))


def _ffn_kernel(*refs, seq_mode, tiles_per_seq):
    if seq_mode:
        (x_ref, yg_ref, yn_ref, wo_ref, nf_ref, wup_ref, cw_ref, cb_ref, wdn_ref, nl_ref,
         y_ref, conv_ref, abuf) = refs
    else:
        (x_ref, yg_ref, yn_ref, wo_ref, nf_ref, wup_ref, cw_ref, cb_ref, wdn_ref, nl_ref, prev_ref,
         y_ref, conv_ref) = refs
    tm = x_ref.shape[0]
    dff = wdn_ref.shape[0]
    gw = yg_ref.shape[1]
    x1 = x_ref[...] + _dot(yg_ref[...], wo_ref[0:gw, :]) + _dot(yn_ref[...], wo_ref[gw:, :])
    h2 = _rms(x1, nf_ref[...]).astype(BF16)
    if seq_mode:
        t_in = pl.program_id(0) % tiles_per_seq

        @pl.when(t_in == 0)
        def _():
            abuf[0:8, :] = jnp.zeros((8, dff), F32)

    acc = jnp.zeros(x1.shape, F32)
    for c0 in range(0, dff, FFN_CHUNK):
        cs = slice(c0, c0 + FFN_CHUNK)
        a = _dot(h2, wup_ref[:, cs])
        bgate = _dot(h2, wup_ref[:, dff + c0:dff + c0 + FFN_CHUNK])
        if seq_mode:
            abuf[8:8 + tm, cs] = a
            am1 = abuf[7:7 + tm, cs]
            am2 = abuf[6:6 + tm, cs]
        else:
            am2 = prev_ref[:, cs]
            am1 = prev_ref[:, dff + c0:dff + c0 + FFN_CHUNK]
            conv_ref[:, cs] = am1
            conv_ref[:, dff + c0:dff + c0 + FFN_CHUNK] = a
        conv = cb_ref[:, cs] + cw_ref[0:1, cs] * am2 + cw_ref[1:2, cs] * am1 + cw_ref[2:3, cs] * a
        act = (_gelu(conv) * bgate).astype(BF16)
        acc = acc + _dot(act, wdn_ref[cs, :])
    if seq_mode:
        @pl.when(t_in == tiles_per_seq - 1)
        def _():
            conv_ref[0] = abuf[8 + tm - (CONV_W - 1):8 + tm, :]

        abuf[0:8, :] = abuf[tm:tm + 8, :]
    y_ref[...] = _rms(x1 + acc, nl_ref[...])


def _ffn(x2, yg, yn, wo, nf, wup, cw, cb, wdn, nl, prev, b, t, tm):
    n, d = x2.shape
    dff = wdn.shape[0]
    seq_mode = prev is None
    row = lambda w: pl.BlockSpec((tm, w), lambda i: (i, 0))
    in_specs = [row(d), row(yg.shape[1]), row(yn.shape[1]), _const_spec(wo.shape), _const_spec((1, d)),
                _const_spec(wup.shape), _const_spec(cw.shape), _const_spec((1, dff)), _const_spec(wdn.shape),
                _const_spec((1, d))]
    args = [x2, yg, yn, wo, nf, wup, cw, cb, wdn, nl]
    if seq_mode:
        tps = t // tm
        conv_shape = jax.ShapeDtypeStruct((b, CONV_W - 1, dff), F32)
        conv_spec = pl.BlockSpec((1, CONV_W - 1, dff), lambda i: (i // tps, 0, 0))
        scratch = [pltpu.VMEM((8 + tm, dff), F32)]
    else:
        tps = 1
        in_specs.append(row(2 * dff))
        args.append(prev)
        conv_shape = jax.ShapeDtypeStruct((n, 2 * dff), F32)
        conv_spec = row(2 * dff)
        scratch = []
    return pl.pallas_call(
        functools.partial(_ffn_kernel, seq_mode=seq_mode, tiles_per_seq=tps), grid=(n // tm,),
        in_specs=in_specs, out_specs=[row(d), conv_spec],
        out_shape=[jax.ShapeDtypeStruct((n, d), F32), conv_shape], scratch_shapes=scratch,
        compiler_params=pltpu.CompilerParams(dimension_semantics=("arbitrary",), vmem_limit_bytes=VMEM_LIMIT),
        name="ffn_seq" if seq_mode else "ffn_step")(*args)


def _pack_in_weights(w_in, w_gla_a, b_gla_a, b_nsa_gate):
    sizes = (GQ_W, GQ_W, GV_W, GLA_RANK, GV_W, NQ_W, KV_W, KV_W, KV_W, 3 * NSA_HEADS)
    offs = np.concatenate([[0], np.cumsum(sizes)])
    part = lambda i: w_in[:, offs[i]:offs[i + 1]]
    wm = jnp.concatenate([part(0), part(1), part(2), part(4), part(5), part(6), part(7), part(8)], axis=1).astype(BF16)
    small = jnp.concatenate([part(3), part(9)], axis=1)
    ws = jnp.pad(small, ((0, 0), (0, 128 - small.shape[1]))).astype(BF16)
    wa = jnp.pad(w_gla_a, ((0, 128 - GLA_RANK), (0, 0))).astype(BF16)
    ba = b_gla_a.reshape(1, GQ_W).astype(F32)
    bg = jnp.pad(b_nsa_gate.reshape(1, -1), ((0, 0), (GATE_OFF, 128 - GATE_OFF - 3 * NSA_HEADS))).astype(F32)
    return wm, ws, wa, ba, bg


def kernel(x_prompt, x_sample, cache_kv_cmp, cache_kv_sel, cache_kv_win, state_gla, state_conv, page_table,
           norm_mix, w_in, w_gla_a, b_gla_a, gla_norm, b_nsa_gate, w_out, norm_ffn, w_ffn_up, conv_w, conv_b,
           w_ffn_down, norm_final):
    bp, seq, d = x_prompt.shape
    db, dseq, _ = x_sample.shape
    assert dseq == 1 and norm_mix.shape[0] == 1
    npg = page_table.shape[1]
    psz = cache_kv_cmp.shape[2]
    past = npg * psz
    n_phys = cache_kv_cmp.shape[1]
    w_buf = cache_kv_win.shape[2]
    dff = w_ffn_down.shape[1]
    assert past % L_SEL == 0 and w_buf == WINDOW and past >= WINDOW and seq >= WINDOW + Q_BLOCK
    tm = min(512, seq)

    wm, ws, wa, ba, bg = _pack_in_weights(w_in[0], w_gla_a[0], b_gla_a[0], b_nsa_gate[0])
    nmix = norm_mix[0].reshape(1, d).astype(F32)
    wo = w_out[0].astype(BF16)
    nf = norm_ffn[0].reshape(1, d).astype(F32)
    wup = w_ffn_up[0].astype(BF16)
    cw = conv_w[0].astype(F32)
    cb = conv_b[0].reshape(1, dff).astype(F32)
    wdn = w_ffn_down[0].astype(BF16)
    nl = norm_final.reshape(1, d).astype(F32)

    (gq, gk, gv, gr, glog, kvc, kvs, kvw, qa, ksa, kwa, vst, vwt, cm, gates_t) = _proj(
        x_prompt, nmix, wm, ws, wa, ba, bg, tm, True)
    y_gla, s_p = _gla_prompt(gq, gk, gv, glog, gr, gla_norm[0], bp, seq, tm)
    cm_perm = jnp.concatenate([cm[:, 0::2], cm[:, 1::2]], axis=1)
    oc, selm, anyf = _nsa_cmp(qa, cm_perm)
    bits = _tile_bits(anyf, seq // L_SEL)
    y_nsa = _nsa_sel(bits, qa, ksa, kwa, vst, vwt, selm, oc, gates_t)
    y_p, conv_p = _ffn(x_prompt.reshape(bp * seq, d), y_gla, y_nsa, wo, nf, wup, cw, cb, wdn, nl, None, bp, seq, tm)
    w_keep = min(WINDOW, seq)
    kv_shape = lambda n_b, n_t: (1, n_b, n_t, 2, NSA_G, NSA_DH)
    out_prompt = (y_p.reshape(bp, seq, d),
                  kvc.reshape(kv_shape(bp, seq)), kvs.reshape(kv_shape(bp, seq)),
                  kvw.reshape(bp, seq, KV_W)[:, seq - w_keep:].reshape(kv_shape(bp, w_keep)),
                  s_p[None], conv_p[None])

    xs3 = x_sample.reshape(1, db, d)
    (gq, gk, gv, gr, glog, kvc_s, kvs_s, kvw_s, nqh, gates) = _proj(xs3, nmix, wm, ws, wa, ba, bg, db, False)
    y_gla_s, s_s = _gla_step(gq, gk, glog, gv, gr, gla_norm[0], state_gla[0].astype(F32))
    q3 = nqh[0].transpose(1, 0, 2)
    cm_s = _page_means(cache_kv_cmp[0].reshape(n_phys, psz, KV_W), page_table)
    k_sel = min(N_SEL, past // L_SEL + 1) - 1
    oc_s, idx = _nsa_cmp_step(q3, cm_s, past, k_sel)
    yn_s, win_out = _nsa_sel_step(page_table, idx, q3, oc_s, gates.reshape(db, 1, 128), kvs_s.reshape(db, 1, KV_W),
                                  kvw_s.reshape(db, 1, KV_W), cache_kv_win[0].reshape(db, w_buf, KV_W),
                                  cache_kv_sel[0].reshape(n_phys, psz, KV_W), past, k_sel)
    y_s, conv_s = _ffn(x_sample.reshape(db, d), y_gla_s, yn_s.reshape(db, NQ_W), wo, nf, wup, cw, cb, wdn, nl,
                       state_conv[0].reshape(db, (CONV_W - 1) * dff).astype(F32), db, 1, db)
    out_sample = (y_s.reshape(db, 1, d),
                  kvc_s.reshape(kv_shape(db, 1)), kvs_s.reshape(kv_shape(db, 1)),
                  win_out.reshape(kv_shape(db, w_buf)),
                  s_s.reshape(1, db, GLA_HEADS, GLA_DK, GLA_DV), conv_s.reshape(1, db, CONV_W - 1, dff))
    return (out_prompt[0], out_sample[0]) + out_prompt[1:] + out_sample[1:]
```

```python
import functools

import numpy as np
import jax
import jax.numpy as jnp
from jax import lax
from jax.experimental import pallas as pl
from jax.experimental.pallas import tpu as pltpu

F32 = jnp.float32
BF16 = jnp.bfloat16

GLA_HEADS = 8
GLA_DK = 32
GLA_DV = 64
GLA_RANK = 16
GLA_TAU = 16.0
GLA_CHUNK = 64
NSA_HEADS = 8
NSA_G = 2
NSA_REP = 4
NSA_DH = 64
L_CMP = 32
L_SEL = 64
N_SEL = 16
WINDOW = 512
Q_BLOCK = 128
CONV_W = 3
EPS = 1e-6
NEG = -1e30
SLOPES = tuple(float(2.0 ** -(h + 1)) for h in range(NSA_HEADS))

GQ_W = GLA_HEADS * GLA_DK
GV_W = GLA_HEADS * GLA_DV
NQ_W = NSA_HEADS * NSA_DH
KV_W = 2 * NSA_G * NSA_DH
GATE_OFF = GLA_RANK
SEL_TILE = 256
VMEM_LIMIT = 56 * 1024 * 1024


def _nt(a, b):
    return lax.dot_general(a, b, (((1,), (1,)), ((), ())), preferred_element_type=F32)


def _tn(a, b):
    return lax.dot_general(a, b, (((0,), (0,)), ((), ())), preferred_element_type=F32)


def _dot(a, b):
    return jnp.dot(a, b, preferred_element_type=F32)


def _split3(x):
    hi = x.astype(BF16)
    r1 = x - hi.astype(F32)
    mid = r1.astype(BF16)
    lo = (r1 - mid.astype(F32)).astype(BF16)
    return hi, mid, lo


def _dot_exact_lhs(m_bf16, x):
    hi, mid, lo = _split3(x)
    return _dot(m_bf16, hi) + _dot(m_bf16, mid) + _dot(m_bf16, lo)


def _sigmoid(x):
    return 1.0 / (1.0 + jnp.exp(-x))


def _log_sigmoid(x):
    return jnp.minimum(x, 0.0) - jnp.log(1.0 + jnp.exp(-jnp.abs(x)))


def _rms(x, g):
    ms = jnp.mean(x * x, axis=-1, keepdims=True)
    return x * lax.rsqrt(ms + EPS) * g


def _const_spec(shape):
    nd = len(shape)
    return pl.BlockSpec(shape, lambda *_: (0,) * nd, pipeline_mode=pl.Buffered(1))


def _lane_halves(pair):
    return pair, pltpu.roll(pair, NSA_DH, 1)


def _proj_kernel(x_ref, g_ref, wm_ref, ws_ref, wa_ref, ba_ref, bg_ref,
                 gq_ref, gk_ref, gv_ref, gr_ref, glog_ref, kvc_ref, kvs_ref, kvw_ref, *attn_refs, prompt):
    tm = x_ref.shape[0]
    h = _rms(x_ref[...], g_ref[...]).astype(BF16)

    def mm(a, b):
        return _dot(h, wm_ref[:, a:b])

    o = 0
    gq_ref[...] = mm(o, o + GQ_W) * (GLA_DK ** -0.5); o += GQ_W
    gk_ref[...] = mm(o, o + GQ_W); o += GQ_W
    gv_ref[...] = mm(o, o + GV_W); o += GV_W
    gr_ref[...] = mm(o, o + GV_W); o += GV_W
    nq = mm(o, o + NQ_W) * (NSA_DH ** -0.5); o += NQ_W
    kvc = mm(o, o + KV_W); o += KV_W
    kvs = mm(o, o + KV_W); o += KV_W
    kvw = mm(o, o + KV_W); o += KV_W
    kvs_t = kvs.T
    kvw_t = kvw.T
    kvc_ref[0] = kvc.T
    kvs_ref[0] = kvs_t
    kvw_ref[0] = kvw_t
    small = _dot(h, ws_ref[...])
    z = _dot(small.astype(BF16), wa_ref[...]) + ba_ref[...]
    glog_ref[...] = _log_sigmoid(z) * (1.0 / GLA_TAU)
    gates = _sigmoid(small + bg_ref[...])
    if not prompt:
        nqh_ref, gates_ref, kvs_rows_ref, kvw_rows_ref = attn_refs
        for hh in range(NSA_HEADS):
            nqh_ref[0, hh] = nq[:, hh * NSA_DH:(hh + 1) * NSA_DH].astype(BF16)
        gates_ref[...] = gates
        kvs_rows_ref[...] = kvs
        kvw_rows_ref[...] = kvw
        return
    qa_ref, ksa_ref, kwa_ref, vst_ref, vwt_ref, cm_ref, gt_ref = attn_refs
    lane = lax.broadcasted_iota(jnp.int32, (tm, 128), 1)
    low = lane < NSA_DH
    for j in range(NSA_HEADS // 2):
        for hh, qh in zip((2 * j, 2 * j + 1), _lane_halves(nq[:, j * 128:(j + 1) * 128])):
            qa_ref[0, hh] = jnp.where(low, qh, jnp.where(lane == NSA_DH, SLOPES[hh], 0.0)).astype(BF16)
    pos = pl.program_id(1) * tm + lax.broadcasted_iota(jnp.int32, (tm, 128), 0)
    tile_pos = (pos % SEL_TILE).astype(F32)
    for src, src_t, ka_ref, vt_ref, tile in ((kvs, kvs_t, ksa_ref, vst_ref, SEL_TILE),
                                             (kvw, kvw_t, kwa_ref, vwt_ref, Q_BLOCK)):
        for g, kg in enumerate(_lane_halves(src[:, 0:NSA_G * NSA_DH])):
            ka_ref[0, g] = jnp.where(low, kg, tile_pos).astype(BF16)
        vt = src_t[NSA_G * NSA_DH:, :].astype(BF16)
        for g in range(NSA_G):
            for c in range(tm // tile):
                vt_ref[0, g, c] = vt[g * NSA_DH:(g + 1) * NSA_DH, c * tile:(c + 1) * tile]
    cm_ref[0] = kvc.reshape(tm // L_CMP, L_CMP, KV_W).sum(axis=1) * (1.0 / L_CMP)
    gt_ref[...] = gates.T


def _proj(x3, norm_g, wm, ws, wa, ba, bg, tm, prompt):
    b, t, d = x3.shape
    n = b * t
    nt = t // tm
    x2 = x3.reshape(n, d)
    row = lambda w: pl.BlockSpec((tm, w), lambda bi, ti: (bi * nt + ti, 0))
    sds = jax.ShapeDtypeStruct
    kvt = pl.BlockSpec((1, KV_W, tm), lambda bi, ti: (bi, 0, ti))
    out_shape = [sds((n, GQ_W), F32), sds((n, GQ_W), F32), sds((n, GV_W), F32), sds((n, GV_W), F32),
                 sds((n, GQ_W), F32), sds((b, KV_W, t), F32), sds((b, KV_W, t), F32), sds((b, KV_W, t), F32)]
    out_specs = [row(GQ_W), row(GQ_W), row(GV_W), row(GV_W), row(GQ_W), kvt, kvt, kvt]
    if prompt:
        hm = lambda nh: pl.BlockSpec((1, nh, tm, 128), lambda bi, ti: (bi, 0, ti, 0))
        vt = lambda tile: pl.BlockSpec((1, NSA_G, tm // tile, NSA_DH, tile), lambda bi, ti: (bi, 0, ti, 0, 0))
        out_shape += [sds((b, NSA_HEADS, t, 128), BF16), sds((b, NSA_G, t, 128), BF16), sds((b, NSA_G, t, 128), BF16),
                      sds((b, NSA_G, t // SEL_TILE, NSA_DH, SEL_TILE), BF16),
                      sds((b, NSA_G, t // Q_BLOCK, NSA_DH, Q_BLOCK), BF16),
                      sds((b, t // L_CMP, KV_W), F32), sds((128, n), F32)]
        out_specs += [hm(NSA_HEADS), hm(NSA_G), hm(NSA_G), vt(SEL_TILE), vt(Q_BLOCK),
                      pl.BlockSpec((1, tm // L_CMP, KV_W), lambda bi, ti: (bi, ti, 0)),
                      pl.BlockSpec((128, tm), lambda bi, ti: (0, bi * nt + ti))]
    else:
        out_shape += [sds((b, NSA_HEADS, t, NSA_DH), BF16), sds((n, 128), F32), sds((n, KV_W), F32), sds((n, KV_W), F32)]
        out_specs += [pl.BlockSpec((1, NSA_HEADS, tm, NSA_DH), lambda bi, ti: (bi, 0, ti, 0)), row(128),
                      row(KV_W), row(KV_W)]
    in_specs = [row(d), _const_spec((1, d)), _const_spec(wm.shape), _const_spec(ws.shape),
                _const_spec(wa.shape), _const_spec(ba.shape), _const_spec(bg.shape)]
    return pl.pallas_call(
        functools.partial(_proj_kernel, prompt=prompt), grid=(b, nt), in_specs=in_specs, out_specs=out_specs,
        out_shape=out_shape,
        compiler_params=pltpu.CompilerParams(dimension_semantics=("arbitrary", "arbitrary"),
                                             vmem_limit_bytes=VMEM_LIMIT),
        name="proj")(x2, norm_g, wm, ws, wa, ba, bg)


def _gla_consts():
    c = GLA_CHUNK
    t = np.arange(c)
    s = t[None, :]
    mats = [np.tril(np.ones((c, c), np.float32))]
    mqs, mks, masks = [], [], []
    m = c // 2
    while m >= 1:
        blk = t // (2 * m)
        mid = blk * 2 * m + m - 1
        upper = t > mid
        mqs.append((upper[:, None] & (s > mid[:, None]) & (s <= t[:, None])).astype(np.float32))
        mks.append(((~upper)[:, None] & (s > t[:, None]) & (s <= mid[:, None])).astype(np.float32))
        masks.append((upper[:, None] & (~upper)[None, :] & (blk[:, None] == blk[None, :])).astype(np.float32))
        m //= 2
    mcat = np.concatenate(mats + mqs + mks, axis=0)
    lvl = np.stack([np.tile(mk, (1, GLA_HEADS)) for mk in masks])
    eye = np.tile(np.eye(c, dtype=np.float32), (1, GLA_HEADS))
    hk = np.arange(GLA_HEADS * GLA_DK) // GLA_DK
    hc = np.arange(GLA_HEADS * c) // c
    hv = np.arange(GLA_HEADS * GLA_DV) // GLA_DV
    bdk = (hc[:, None] == hk[None, :]).astype(np.float32)
    bdv = (hc[:, None] == hv[None, :]).astype(np.float32)
    bds = (hv[:, None] == hk[None, :]).astype(np.float32)
    hsum = (hk[:, None] == hc[None, :]).astype(np.float32)
    hmean = (hv[:, None] == hv[None, :]).astype(np.float32) / GLA_DV
    return (jnp.asarray(mcat, BF16), jnp.asarray(lvl, F32), jnp.asarray(eye, F32), jnp.asarray(bdk, BF16),
            jnp.asarray(bdv, BF16), jnp.asarray(bds, F32), jnp.asarray(hsum, BF16), jnp.asarray(hmean, BF16))


def _gla_kernel(q_ref, k_ref, v_ref, g_ref, r_ref, gain_ref, mcat_ref, lvl_ref, eye_ref, bdk_ref, bdv_ref,
                bds_ref, hsum_ref, hmean_ref, y_ref, st_ref, s_sc):
    c = GLA_CHUNK
    nlev = lvl_ref.shape[0]
    n_chunks = q_ref.shape[0] // c

    @pl.when(pl.program_id(1) == 0)
    def _():
        s_sc[...] = jnp.zeros_like(s_sc)

    def chunk(ci):
        r0 = pl.multiple_of(ci * c, c)
        q = q_ref[pl.ds(r0, c), :]
        k = k_ref[pl.ds(r0, c), :]
        v = v_ref[pl.ds(r0, c), :].astype(BF16)
        g = g_ref[pl.ds(r0, c), :]
        e_all = _dot_exact_lhs(mcat_ref[...], g)
        b = e_all[0:c]
        a = _dot((q * k).astype(BF16), hsum_ref[...]) * eye_ref[...]
        for lv in range(nlev):
            eq = e_all[(1 + lv) * c:(2 + lv) * c]
            ek = e_all[(1 + nlev + lv) * c:(2 + nlev + lv) * c]
            qt = (q * jnp.exp(eq)).astype(BF16)
            kt = (k * jnp.exp(ek)).astype(BF16)
            kbd = jnp.concatenate([kt] * GLA_HEADS, axis=0) * bdk_ref[...]
            a = a + _nt(qt, kbd) * lvl_ref[lv]
        vbd = jnp.concatenate([v] * GLA_HEADS, axis=0) * bdv_ref[...]
        s_t = s_sc[...]
        o = _dot(a.astype(BF16), vbd) + _nt((q * jnp.exp(b)).astype(BF16), s_t.astype(BF16))
        b_last = b[c - 1:c, :]
        kd = (k * jnp.exp(b_last - b)).astype(BF16)
        s_sc[...] = s_t * jnp.exp(b_last) + _tn(v, kd) * bds_ref[...]
        o2 = o * o
        o2h = o2.astype(BF16)
        o2l = (o2 - o2h.astype(F32)).astype(BF16)
        ms = _dot(o2h, hmean_ref[...]) + _dot(o2l, hmean_ref[...])
        r = r_ref[pl.ds(r0, c), :]
        y = o * lax.rsqrt(ms + EPS) * gain_ref[...] * (r * _sigmoid(r))
        y_ref[pl.ds(r0, c), :] = y.astype(BF16)

    def chunk_pair(pi, carry):
        chunk(2 * pi)
        chunk(2 * pi + 1)
        return carry

    lax.fori_loop(0, n_chunks // 2, chunk_pair, 0)

    @pl.when(pl.program_id(1) == pl.num_programs(1) - 1)
    def _():
        st_ref[0] = s_sc[...]


def _gla_prompt(gq, gk, gv, glog, gr, gain, b, t, tc):
    n = b * t
    nt = t // tc
    consts = _gla_consts()
    gain_t = jnp.tile(gain.astype(F32).reshape(1, GLA_DV), (1, GLA_HEADS))
    row = lambda w: pl.BlockSpec((tc, w), lambda bi, ti: (bi * nt + ti, 0))
    in_specs = [row(GQ_W), row(GQ_W), row(GV_W), row(GQ_W), row(GV_W), _const_spec((1, GV_W))]
    in_specs += [_const_spec(a.shape) for a in consts]
    y, st = pl.pallas_call(
        _gla_kernel, grid=(b, nt), in_specs=in_specs,
        out_specs=[row(GV_W), pl.BlockSpec((1, GV_W, GQ_W), lambda bi, ti: (bi, 0, 0))],
        out_shape=[jax.ShapeDtypeStruct((n, GV_W), BF16), jax.ShapeDtypeStruct((b, GV_W, GQ_W), F32)],
        scratch_shapes=[pltpu.VMEM((GV_W, GQ_W), F32)],
        compiler_params=pltpu.CompilerParams(dimension_semantics=("arbitrary", "arbitrary"),
                                             vmem_limit_bytes=VMEM_LIMIT),
        name="gla_prompt")(gq, gk, gv, glog, gr, gain_t, *consts)
    s = jnp.stack([st[:, h * GLA_DV:(h + 1) * GLA_DV, h * GLA_DK:(h + 1) * GLA_DK] for h in range(GLA_HEADS)], axis=1)
    return y, s.transpose(0, 1, 3, 2)


def _gla_step_kernel(q_ref, k_ref, g_ref, v_ref, r_ref, gain_ref, s0_ref, y_ref, s1_ref):
    bb = q_ref.shape[0]
    hd = GLA_HEADS * GLA_DK
    ri = lax.broadcasted_iota(jnp.int32, (hd, hd), 0)
    ci = lax.broadcasted_iota(jnp.int32, (hd, hd), 1)
    eye = ri == ci
    rowhead = lax.broadcasted_iota(jnp.int32, (hd, GLA_DV), 0) // GLA_DK

    def col(x_row):
        return jnp.sum(jnp.where(eye, jnp.broadcast_to(x_row, (hd, hd)), 0.0), axis=1, keepdims=True)

    for i in range(bb):
        q = q_ref[i:i + 1, :]
        k = k_ref[i:i + 1, :]
        eg = jnp.exp(g_ref[i:i + 1, :])
        v3 = v_ref[i]
        s0 = s0_ref[i]
        vrows = jnp.zeros((hd, GLA_DV), F32)
        for h in range(GLA_HEADS):
            vrows = jnp.where(rowhead == h, jnp.broadcast_to(v3[h:h + 1, :], (hd, GLA_DV)), vrows)
        s1_ref[i] = col(eg) * s0 + col(k) * vrows
        t1 = (col(q * eg) * s0).reshape(GLA_HEADS, GLA_DK, GLA_DV).sum(axis=1)
        qk = col(q * k).reshape(GLA_HEADS, GLA_DK, 1).sum(axis=1)
        o = t1 + qk * v3
        ms = jnp.mean(o * o, axis=-1, keepdims=True)
        r3 = r_ref[i]
        y_ref[i] = (o * lax.rsqrt(ms + EPS) * gain_ref[...] * (r3 * _sigmoid(r3))).astype(BF16)


def _gla_step(gq, gk, glog, gv, gr, gain, s0, bb=8):
    n = gq.shape[0]
    hd = GLA_HEADS * GLA_DK
    v3 = gv.reshape(n, GLA_HEADS, GLA_DV)
    r3 = gr.reshape(n, GLA_HEADS, GLA_DV)
    s0r = s0.reshape(n, hd, GLA_DV)
    row = pl.BlockSpec((bb, hd), lambda i: (i, 0))
    h3 = pl.BlockSpec((bb, GLA_HEADS, GLA_DV), lambda i: (i, 0, 0))
    st = pl.BlockSpec((bb, hd, GLA_DV), lambda i: (i, 0, 0))
    y, s1 = pl.pallas_call(
        _gla_step_kernel, grid=(n // bb,),
        in_specs=[row, row, row, h3, h3, _const_spec((1, GLA_DV)), st],
        out_specs=[h3, st],
        out_shape=[jax.ShapeDtypeStruct((n, GLA_HEADS, GLA_DV), BF16), jax.ShapeDtypeStruct((n, hd, GLA_DV), F32)],
        compiler_params=pltpu.CompilerParams(dimension_semantics=("arbitrary",)),
        name="gla_step")(gq, gk, glog, v3, r3, gain.astype(F32).reshape(1, GLA_DV), s0r)
    return y.reshape(n, GV_W), s1


def _topk_mask_t(v, k_sel):
    n = v.shape[0]
    sb = lax.broadcasted_iota(jnp.int32, v.shape, 0)
    sel = jnp.zeros(v.shape, F32)
    for _ in range(k_sel):
        m = jnp.max(v, axis=0, keepdims=True)
        idx = jnp.min(jnp.where(v == m, sb, n), axis=0, keepdims=True)
        hit = sb == idx
        sel = jnp.where(hit, 1.0, sel)
        v = jnp.where(hit, -jnp.inf, v)
    return sel


def _topk_mask(v, k_sel):
    n = v.shape[-1]
    sb = lax.broadcasted_iota(jnp.int32, v.shape, 1)
    sel = jnp.zeros(v.shape, F32)
    idxs = []
    for _ in range(k_sel):
        m = jnp.max(v, axis=-1, keepdims=True)
        idx = jnp.min(jnp.where(v == m, sb, n), axis=-1, keepdims=True)
        hit = sb == idx
        sel = jnp.where(hit, 1.0, sel)
        v = jnp.where(hit, -jnp.inf, v)
        idxs.append(idx)
    return sel, idxs


def _group_queries(q_ref, g):
    return jnp.concatenate([q_ref[0, g * NSA_REP + r] for r in range(NSA_REP)], axis=0)


def _nsa_cmp_kernel(q_ref, cm_ref, oc_ref, selm_ref, any_ref):
    nc = cm_ref.shape[1]
    half = nc // 2
    ns = half
    q0 = pl.program_id(1) * Q_BLOCK
    qpos = q0 + lax.broadcasted_iota(jnp.int32, (1, Q_BLOCK), 1)
    row = lax.broadcasted_iota(jnp.int32, (nc, 1), 0)
    blk = jnp.where(row < half, 2 * row, 2 * (row - half) + 1)
    mask_c = ((blk + 1) * L_CMP - 1) <= qpos
    dist_c = qpos.astype(F32) - (blk.astype(F32) * L_CMP + (L_CMP - 1) / 2.0)
    sb = lax.broadcasted_iota(jnp.int32, (ns, 1), 0)
    cur = qpos // L_SEL
    forced = (sb == 0) | (sb == cur) | (sb == cur - 1)
    future = sb * L_SEL > qpos
    cm = cm_ref[0]
    low = lax.broadcasted_iota(jnp.int32, (nc, 128), 1) < NSA_DH
    vt_all = cm[:, NSA_G * NSA_DH:].T
    ones = jnp.ones((8, Q_BLOCK), BF16)
    for g, kg in enumerate(_lane_halves(cm[:, 0:NSA_G * NSA_DH])):
        kc = jnp.where(low, kg, 0.0).astype(BF16)
        vct = vt_all[g * NSA_DH:(g + 1) * NSA_DH, :].astype(BF16)
        s_all = _nt(kc, _group_queries(q_ref, g))
        imp = jnp.zeros((nc, Q_BLOCK), F32)
        ps = []
        for r in range(NSA_REP):
            h = g * NSA_REP + r
            s = s_all[:, r * Q_BLOCK:(r + 1) * Q_BLOCK] - SLOPES[h] * dist_c
            s = jnp.where(mask_c, s, NEG)
            e = jnp.exp(s - jnp.max(s, axis=0, keepdims=True))
            p = jnp.where(mask_c, e / jnp.sum(e, axis=0, keepdims=True), 0.0)
            ps.append(p.astype(BF16))
            imp = imp + p
        oct = _dot(vct, jnp.concatenate(ps, axis=1))
        for r in range(NSA_REP):
            oc_ref[0, g * NSA_REP + r] = oct[:, r * Q_BLOCK:(r + 1) * Q_BLOCK]
        imp2 = imp[:half] + imp[half:]
        v = jnp.where(future, -jnp.inf, jnp.where(forced, jnp.inf, imp2))
        sel = _topk_mask_t(v, min(N_SEL, ns))
        selm_ref[0, g] = sel
        any_ref[0, g:g + 1, :] = _nt(ones, sel.astype(BF16))[0:1]


def _nsa_cmp(qa, cm_perm):
    b, _, t, _ = qa.shape
    nqb = t // Q_BLOCK
    nc = t // L_CMP
    ns = t // L_SEL
    return pl.pallas_call(
        _nsa_cmp_kernel, grid=(b, nqb),
        in_specs=[pl.BlockSpec((1, NSA_HEADS, Q_BLOCK, 128), lambda bi, i: (bi, 0, i, 0)),
                  pl.BlockSpec((1, nc, KV_W), lambda bi, i: (bi, 0, 0))],
        out_specs=[pl.BlockSpec((1, NSA_HEADS, NSA_DH, Q_BLOCK), lambda bi, i: (bi, 0, 0, i)),
                   pl.BlockSpec((1, NSA_G, ns, Q_BLOCK), lambda bi, i: (bi, 0, 0, i)),
                   pl.BlockSpec((1, NSA_G, ns), lambda bi, i: (bi * nqb + i, 0, 0))],
        out_shape=[jax.ShapeDtypeStruct((b, NSA_HEADS, NSA_DH, t), F32),
                   jax.ShapeDtypeStruct((b, NSA_G, ns, t), F32),
                   jax.ShapeDtypeStruct((b * nqb, NSA_G, ns), F32)],
        compiler_params=pltpu.CompilerParams(dimension_semantics=("arbitrary", "arbitrary"),
                                             vmem_limit_bytes=VMEM_LIMIT),
        name="nsa_cmp")(qa, cm_perm)


WIN_TILES = (WINDOW + Q_BLOCK) // Q_BLOCK


def _nsa_sel_kernel(bits_ref, q_ref, ksa_ref, kwa_ref, vst_ref, vwt_ref, selm_ref, oc_ref, gt_ref, y_ref,
                    m_sc, l_sc, acc_sc):
    t_len = ksa_ref.shape[2]
    nqb = pl.num_programs(1)
    bi = pl.program_id(0)
    i = pl.program_id(1)
    q0 = i * Q_BLOCK
    qpos = q0 + lax.broadcasted_iota(jnp.int32, (1, Q_BLOCK), 1)
    diag = q0 // SEL_TILE
    nwords = (t_len // SEL_TILE + 15) // 16
    per = SEL_TILE // L_SEL
    wt0 = jnp.clip(i - WINDOW // Q_BLOCK, 0, t_len // Q_BLOCK - WIN_TILES)
    wstart = pl.multiple_of(wt0 * Q_BLOCK, Q_BLOCK)
    wkeys = WIN_TILES * Q_BLOCK
    wdist = qpos - (wstart + lax.broadcasted_iota(jnp.int32, (wkeys, 1), 0))
    wvalid = (wdist >= 0) & (wdist <= WINDOW)
    kiota = lax.broadcasted_iota(jnp.int32, (SEL_TILE, 1), 0)
    gts = gt_ref[...]
    q_alls = [_group_queries(q_ref, g) for g in range(NSA_G)]
    m_sc[...] = jnp.full(m_sc.shape, NEG, F32)
    l_sc[...] = jnp.zeros_like(l_sc)
    acc_sc[...] = jnp.zeros_like(acc_sc)
    base = (bi * nqb + i) * NSA_G * nwords

    def attend_tile(g, t, causal):
        ks = pl.multiple_of(t * SEL_TILE, SEL_TILE)
        s_all = _nt(ksa_ref[0, g, pl.ds(ks, SEL_TILE), :], q_alls[g])
        mk = jnp.concatenate([jnp.broadcast_to(selm_ref[0, g, pl.ds(t * per + j, 1), :], (L_SEL, Q_BLOCK))
                              for j in range(per)], axis=0) > 0.5
        if causal:
            mk = mk & ((ks + kiota) <= qpos)
        shift = (ks - q0).astype(F32)
        s4 = jnp.concatenate(
            [jnp.where(mk, s_all[:, r * Q_BLOCK:(r + 1) * Q_BLOCK] + SLOPES[g * NSA_REP + r] * shift, NEG)
             for r in range(NSA_REP)], axis=1)
        m_old = m_sc[g]
        m_new = jnp.maximum(m_old, jnp.max(s4, axis=0, keepdims=True))
        alpha = jnp.exp(m_old - m_new)
        p = jnp.exp(s4 - m_new)
        l_sc[g] = alpha * l_sc[g] + jnp.sum(p, axis=0, keepdims=True)
        acc_sc[g] = alpha * acc_sc[g] + _dot(vst_ref[0, g, t], p.astype(BF16))
        m_sc[g] = m_new

    def tile(t, carry):
        flag = bits_ref[base + t // 16]
        for g in range(1, NSA_G):
            flag = flag | bits_ref[base + g * nwords + t // 16]

        @pl.when((lax.shift_right_logical(flag, t % 16) & 1) == 1)
        def _():
            for g in range(NSA_G):
                attend_tile(g, t, False)

        return carry

    lax.fori_loop(0, diag, tile, 0)
    for g in range(NSA_G):
        attend_tile(g, diag, True)

    for g in range(NSA_G):
        q_all = q_alls[g]
        o_s = acc_sc[g] / l_sc[g]
        s_all = _nt(kwa_ref[0, g, pl.ds(wstart, wkeys), :], q_all)
        vw = jnp.concatenate([vwt_ref[0, g, wt0 + j] for j in range(WIN_TILES)], axis=1)
        shifts = [(((wstart + j * Q_BLOCK) // SEL_TILE) * SEL_TILE - q0).astype(F32) for j in range(WIN_TILES)]
        ps = []
        for r in range(NSA_REP):
            slope = SLOPES[g * NSA_REP + r]
            s = jnp.concatenate([s_all[j * Q_BLOCK:(j + 1) * Q_BLOCK, r * Q_BLOCK:(r + 1) * Q_BLOCK] + slope * shifts[j]
                                 for j in range(WIN_TILES)], axis=0)
            s = jnp.where(wvalid, s, NEG)
            e = jnp.exp(s - jnp.max(s, axis=0, keepdims=True))
            ps.append(jnp.where(wvalid, e / jnp.sum(e, axis=0, keepdims=True), 0.0).astype(BF16))
        o_w = _dot(vw, jnp.concatenate(ps, axis=1))

        outs = []
        for r in range(NSA_REP):
            h = g * NSA_REP + r
            c0 = GATE_OFF + h * 3
            cs = slice(r * Q_BLOCK, (r + 1) * Q_BLOCK)
            outs.append(gts[c0:c0 + 1, :] * oc_ref[0, h] + gts[c0 + 1:c0 + 2, :] * o_s[:, cs]
                        + gts[c0 + 2:c0 + 3, :] * o_w[:, cs])
        gw = NSA_REP * NSA_DH
        y_ref[:, g * gw:(g + 1) * gw] = jnp.concatenate(outs, axis=0).T.astype(BF16)


def _nsa_sel(bits, qa, ksa, kwa, vst, vwt, selm, oc, gates_t):
    b, _, t, _ = qa.shape
    nqb = t // Q_BLOCK
    ns = t // L_SEL
    whole = lambda a: pl.BlockSpec((1,) + a.shape[1:], lambda bi, i, *_: (bi,) + (0,) * (a.ndim - 1),
                                   pipeline_mode=pl.Buffered(1))
    cols = lambda lead: pl.BlockSpec((1,) + lead + (Q_BLOCK,), lambda bi, i, *_: (bi,) + (0,) * len(lead) + (i,))
    grid_spec = pltpu.PrefetchScalarGridSpec(
        num_scalar_prefetch=1, grid=(b, nqb),
        in_specs=[pl.BlockSpec((1, NSA_HEADS, Q_BLOCK, 128), lambda bi, i, *_: (bi, 0, i, 0)),
                  whole(ksa), whole(kwa), whole(vst), whole(vwt),
                  cols((NSA_G, ns)), cols((NSA_HEADS, NSA_DH)),
                  pl.BlockSpec((128, Q_BLOCK), lambda bi, i, *_: (0, bi * nqb + i))],
        out_specs=[pl.BlockSpec((Q_BLOCK, NQ_W), lambda bi, i, *_: (bi * nqb + i, 0))],
        scratch_shapes=[pltpu.VMEM((NSA_G, 1, NSA_REP * Q_BLOCK), F32), pltpu.VMEM((NSA_G, 1, NSA_REP * Q_BLOCK), F32),
                        pltpu.VMEM((NSA_G, NSA_DH, NSA_REP * Q_BLOCK), F32)])
    return pl.pallas_call(
        _nsa_sel_kernel, grid_spec=grid_spec,
        out_shape=[jax.ShapeDtypeStruct((b * t, NQ_W), BF16)],
        compiler_params=pltpu.CompilerParams(dimension_semantics=("arbitrary", "arbitrary"),
                                             vmem_limit_bytes=VMEM_LIMIT),
        name="nsa_sel")(bits, qa, ksa, kwa, vst, vwt, selm, oc, gates_t)[0]


def _tile_bits(anyf, ns):
    r = anyf.shape[0]
    per = SEL_TILE // L_SEL
    ntile = ns // per
    nwords = (ntile + 15) // 16
    tf = (anyf.reshape(r, NSA_G, ntile, per).max(axis=-1) > 0.5).astype(jnp.int32)
    tf = jnp.pad(tf, ((0, 0), (0, 0), (0, nwords * 16 - ntile))).reshape(r, NSA_G, nwords, 16)
    return jnp.sum(tf << jnp.arange(16, dtype=jnp.int32), axis=-1).reshape(-1).astype(jnp.int32)


PAGES_PER_STEP = 8


def _page_mean_kernel(pt_ref, *refs, steps_per_tile):
    avg_ref, o_ref = refs[-2:]
    j = pl.program_id(1)
    x = jnp.concatenate([refs[p][0].reshape(KV_W, refs[p].shape[-1]) for p in range(PAGES_PER_STEP)], axis=1)
    part = _dot(x.astype(BF16), avg_ref[j % steps_per_tile])

    @pl.when(j % steps_per_tile == 0)
    def _():
        o_ref[0] = part

    @pl.when(j % steps_per_tile != 0)
    def _():
        o_ref[0] += part


def _page_means(cache_t, page_table):
    db, npg = page_table.shape
    psz = cache_t.shape[-1]
    per = psz // L_CMP
    nblk = npg * per
    cols = min(128, nblk)
    step_blocks = PAGES_PER_STEP * per
    steps_per_tile = cols // step_blocks
    nj = npg // PAGES_PER_STEP
    tok = np.arange(PAGES_PER_STEP * psz)
    avg = np.zeros((steps_per_tile, PAGES_PER_STEP * psz, cols), np.float32)
    for s in range(steps_per_tile):
        avg[s, tok, s * step_blocks + tok // L_CMP] = 1.0 / L_CMP

    def spec(p):
        return pl.BlockSpec((1,) + cache_t.shape[1:],
                            lambda bi, j, pt: (pt[bi * npg + j * PAGES_PER_STEP + p], 0, 0, 0, 0))

    grid_spec = pltpu.PrefetchScalarGridSpec(
        num_scalar_prefetch=1, grid=(db, nj),
        in_specs=[spec(p) for p in range(PAGES_PER_STEP)]
        + [pl.BlockSpec(avg.shape, lambda bi, j, pt: (0, 0, 0), pipeline_mode=pl.Buffered(1))],
        out_specs=[pl.BlockSpec((1, KV_W, cols), lambda bi, j, pt: (bi, 0, j // steps_per_tile))])
    return pl.pallas_call(
        functools.partial(_page_mean_kernel, steps_per_tile=steps_per_tile), grid_spec=grid_spec,
        out_shape=[jax.ShapeDtypeStruct((db, KV_W, nblk), F32)],
        compiler_params=pltpu.CompilerParams(dimension_semantics=("arbitrary", "arbitrary")),
        name="page_means")(page_table.reshape(-1), *([cache_t] * PAGES_PER_STEP), jnp.asarray(avg, BF16))[0]


def _slope_col():
    row = lax.broadcasted_iota(jnp.int32, (NSA_HEADS, 1), 0)
    sl = jnp.zeros((NSA_HEADS, 1), F32)
    for h in range(NSA_HEADS):
        sl = jnp.where(row == h, SLOPES[h], sl)
    return row, sl


def _nsa_cmp_step_kernel(q_ref, cm_ref, oc_ref, idx_ref, *, past, k_sel):
    bb = q_ref.shape[0]
    nc = cm_ref.shape[2]
    half = nc // 2
    blk = lax.broadcasted_iota(jnp.int32, (1, nc), 1)
    pair = jnp.where(lax.broadcasted_iota(jnp.int32, (nc, half), 0) // (L_SEL // L_CMP)
                     == lax.broadcasted_iota(jnp.int32, (nc, half), 1), 1.0, 0.0).astype(BF16)
    mask_c = ((blk + 1) * L_CMP - 1) <= past
    dist_c = float(past) - (blk.astype(F32) * L_CMP + (L_CMP - 1) / 2.0)
    row, sl = _slope_col()
    imps = []
    for i in range(bb):
        q8 = q_ref[i]
        oc = jnp.zeros((NSA_HEADS, NSA_DH), F32)
        for g in range(NSA_G):
            kct = cm_ref[i, g * NSA_DH:(g + 1) * NSA_DH, :].astype(BF16)
            vct = cm_ref[i, (NSA_G + g) * NSA_DH:(NSA_G + g + 1) * NSA_DH, :].astype(BF16)
            s = _dot(q8, kct) - sl * dist_c
            s = jnp.where(mask_c, s, NEG)
            e = jnp.exp(s - jnp.max(s, axis=-1, keepdims=True))
            p = jnp.where(mask_c, e / jnp.sum(e, axis=-1, keepdims=True), 0.0)
            ingroup = (row // NSA_REP) == g
            oc = jnp.where(ingroup, _nt(p.astype(BF16), vct), oc)
            hi, mid, lo = _split3(jnp.where(ingroup, p, 0.0))
            imps.append(jnp.sum(_dot(hi, pair) + _dot(mid, pair) + _dot(lo, pair), axis=0, keepdims=True))
        oc_ref[i] = oc
    imp = jnp.concatenate(imps, axis=0)
    sb = lax.broadcasted_iota(jnp.int32, imp.shape, 1)
    forced = (sb == 0) | (sb == half - 1)
    _, idxs = _topk_mask(jnp.where(forced, jnp.inf, imp), k_sel)
    lane = lax.broadcasted_iota(jnp.int32, (bb * NSA_G, k_sel), 1)
    acc = jnp.zeros((bb * NSA_G, k_sel), jnp.int32)
    for t, ix in enumerate(idxs):
        acc = jnp.where(lane == t, ix, acc)
    for i in range(bb):
        idx_ref[i] = acc[i * NSA_G:(i + 1) * NSA_G, :]


def _nsa_cmp_step(q3, cm_s, past, k_sel, bb=8):
    db = q3.shape[0]
    nc = cm_s.shape[2]
    return pl.pallas_call(
        functools.partial(_nsa_cmp_step_kernel, past=past, k_sel=k_sel), grid=(db // bb,),
        in_specs=[pl.BlockSpec((bb, NSA_HEADS, NSA_DH), lambda bi: (bi, 0, 0)),
                  pl.BlockSpec((bb, KV_W, nc), lambda bi: (bi, 0, 0))],
        out_specs=[pl.BlockSpec((bb, NSA_HEADS, NSA_DH), lambda bi: (bi, 0, 0)),
                   pl.BlockSpec((bb, NSA_G, k_sel), lambda bi: (bi, 0, 0))],
        out_shape=[jax.ShapeDtypeStruct((db, NSA_HEADS, NSA_DH), F32),
                   jax.ShapeDtypeStruct((db, NSA_G, k_sel), jnp.int32)],
        compiler_params=pltpu.CompilerParams(dimension_semantics=("arbitrary",), vmem_limit_bytes=VMEM_LIMIT),
        name="nsa_cmp_step")(q3, cm_s)


def _nsa_sel_step_kernel(pt_ref, ix_ref, q_ref, oc_ref, gates_ref, ks_new_ref, kw_new_ref, kwt_ref, win_ref, cache_ref,
                         y_ref, wout_ref, blocks, sems, *, past, k_sel, npg, per_page):
    bi = pl.program_id(0)
    w_len = win_ref.shape[2]
    psz = cache_ref.shape[-1]
    n_sel = NSA_G * k_sel

    def page_copy(a, j):
        n = ix_ref[bi * n_sel + j]
        page = pt_ref[bi * npg + n // per_page]
        return pltpu.make_async_copy(cache_ref.at[page, a, j // k_sel], blocks.at[a, j], sems.at[a, j])

    for a in range(2):
        for j in range(n_sel):
            page_copy(a, j).start()
    row, sl = _slope_col()
    q8 = q_ref[0]
    q8f = q8.astype(F32)
    gts = gates_ref[0]
    gcol = []
    for c in range(3):
        col = jnp.zeros((NSA_HEADS, 1), F32)
        for h in range(NSA_HEADS):
            c0 = GATE_OFF + h * 3 + c
            col = jnp.where(row == h, gts[:, c0:c0 + 1], col)
        gcol.append(col)
    win = win_ref[0]
    ks_new = ks_new_ref[0]
    kw_new = kw_new_ref[0]
    wpos = (past - w_len) + lax.broadcasted_iota(jnp.int32, (1, w_len), 1)
    wdist = past - wpos
    wvalid = (wdist <= WINDOW) & (wpos >= 0)

    def joint_attend(s, valid, k_new, v_new, v_t):
        s_new = jnp.sum(q8f * k_new.astype(BF16).astype(F32), axis=-1, keepdims=True)
        s = jnp.where(valid, s, NEG)
        m = jnp.maximum(jnp.max(s, axis=-1, keepdims=True), s_new)
        e = jnp.where(valid, jnp.exp(s - m), 0.0)
        en = jnp.exp(s_new - m)
        l = jnp.sum(e, axis=-1, keepdims=True) + en
        return _nt((e / l).astype(BF16), v_t) + (en / l).astype(BF16).astype(F32) * v_new.astype(BF16).astype(F32)

    ksl = [slice(g * NSA_DH, (g + 1) * NSA_DH) for g in range(NSA_G)]
    vsl = [slice((NSA_G + g) * NSA_DH, (NSA_G + g + 1) * NSA_DH) for g in range(NSA_G)]
    o_w = []
    for g in range(NSA_G):
        s = _dot(q8, win[ksl[g], :].astype(BF16)) - sl * wdist.astype(F32)
        o_w.append(joint_attend(s, wvalid, kw_new[:, ksl[g]], kw_new[:, vsl[g]], win[vsl[g], :].astype(BF16)))
    kwt = kwt_ref[...]
    new_col = jnp.sum(jnp.where(lax.broadcasted_iota(jnp.int32, kwt.shape, 1) == bi, kwt, 0.0), axis=1, keepdims=True)
    wlane = lax.broadcasted_iota(jnp.int32, (1, w_len), 1)
    wout_ref[0] = jnp.where(wlane == w_len - 1, new_col, pltpu.roll(win, w_len - 1, 1))
    for a in range(2):
        for j in range(n_sel):
            page_copy(a, j).wait()
    lane = lax.broadcasted_iota(jnp.int32, (1, k_sel * psz), 1)
    within = lane % psz
    o = jnp.zeros((NSA_HEADS, NSA_DH), F32)
    for g in range(NSA_G):
        k_t = jnp.concatenate([blocks[0, g * k_sel + t] for t in range(k_sel)], axis=1).astype(BF16)
        v_t = jnp.concatenate([blocks[1, g * k_sel + t] for t in range(k_sel)], axis=1).astype(BF16)
        page_pos = jnp.zeros_like(lane)
        half = jnp.zeros_like(lane)
        for t in range(k_sel):
            n = ix_ref[bi * n_sel + g * k_sel + t]
            here = lane // psz == t
            page_pos = jnp.where(here, (n // per_page) * psz, page_pos)
            half = jnp.where(here, n % per_page, half)
        dist = past - (page_pos + within)
        valid = (within // L_SEL == half) & (dist >= 0)
        s = _dot(q8, k_t) - sl * dist.astype(F32)
        o_s = joint_attend(s, valid, ks_new[:, ksl[g]], ks_new[:, vsl[g]], v_t)
        og = gcol[0] * oc_ref[0] + gcol[1] * o_s + gcol[2] * o_w[g]
        o = jnp.where((row // NSA_REP) == g, og, o)
    y_ref[0] = o.astype(BF16)


def _nsa_sel_step(page_table, idx, q3, oc, gates3, ks_new3, kw_new3, kw_new_t, win_t, cache_t, past, k_sel):
    db, npg = page_table.shape
    w_len = win_t.shape[2]
    psz = cache_t.shape[-1]
    per_page = psz // L_SEL

    b3 = lambda w: pl.BlockSpec((1, 1, w), lambda bi, *_: (bi, 0, 0))
    h3 = pl.BlockSpec((1, NSA_HEADS, NSA_DH), lambda bi, *_: (bi, 0, 0))
    wsp = pl.BlockSpec((1, KV_W, w_len), lambda bi, *_: (bi, 0, 0))
    grid_spec = pltpu.PrefetchScalarGridSpec(
        num_scalar_prefetch=2, grid=(db,),
        in_specs=[h3, h3, b3(128), b3(KV_W), b3(KV_W),
                  pl.BlockSpec(kw_new_t.shape, lambda bi, *_: (0, 0), pipeline_mode=pl.Buffered(1)),
                  wsp, pl.BlockSpec(memory_space=pl.ANY)],
        out_specs=[h3, wsp],
        scratch_shapes=[pltpu.VMEM((2, NSA_G * k_sel, NSA_DH, psz), F32),
                        pltpu.SemaphoreType.DMA((2, NSA_G * k_sel))])
    return pl.pallas_call(
        functools.partial(_nsa_sel_step_kernel, past=past, k_sel=k_sel, npg=npg, per_page=per_page),
        grid_spec=grid_spec,
        out_shape=[jax.ShapeDtypeStruct((db, NSA_HEADS, NSA_DH), BF16), jax.ShapeDtypeStruct((db, KV_W, w_len), F32)],
        compiler_params=pltpu.CompilerParams(dimension_semantics=("arbitrary",), vmem_limit_bytes=VMEM_LIMIT),
        name="nsa_sel_step")(page_table.reshape(-1), idx.reshape(-1), q3, oc, gates3, ks_new3, kw_new3, kw_new_t,
                             win_t, cache_t)


FFN_CHUNK = 256


def _gelu(x):
    return 0.5 * x * (1.0 + lax.erf(x * (2.0 ** -0.5)))


def _ffn_kernel(*refs, seq_mode, tiles_per_seq):
    if seq_mode:
        (x_ref, yg_ref, yn_ref, wo_ref, nf_ref, wup_ref, cw_ref, cb_ref, wdn_ref, nl_ref,
         y_ref, conv_ref, abuf) = refs
    else:
        (x_ref, yg_ref, yn_ref, wo_ref, nf_ref, wup_ref, cw_ref, cb_ref, wdn_ref, nl_ref, prev_ref,
         y_ref, conv_ref) = refs
    tm = x_ref.shape[0]
    dff = wdn_ref.shape[0]
    gw = yg_ref.shape[1]
    x1 = x_ref[...] + _dot(yg_ref[...], wo_ref[0:gw, :]) + _dot(yn_ref[...], wo_ref[gw:, :])
    h2 = _rms(x1, nf_ref[...]).astype(BF16)
    if seq_mode:
        t_in = pl.program_id(0) % tiles_per_seq

        @pl.when(t_in == 0)
        def _():
            abuf[0:8, :] = jnp.zeros((8, dff), F32)

    acc = jnp.zeros(x1.shape, F32)
    for c0 in range(0, dff, FFN_CHUNK):
        cs = slice(c0, c0 + FFN_CHUNK)
        a = _dot(h2, wup_ref[:, cs])
        bgate = _dot(h2, wup_ref[:, dff + c0:dff + c0 + FFN_CHUNK])
        if seq_mode:
            abuf[8:8 + tm, cs] = a
            am1 = abuf[7:7 + tm, cs]
            am2 = abuf[6:6 + tm, cs]
        else:
            am2 = prev_ref[:, cs]
            am1 = prev_ref[:, dff + c0:dff + c0 + FFN_CHUNK]
            conv_ref[:, cs] = am1
            conv_ref[:, dff + c0:dff + c0 + FFN_CHUNK] = a
        conv = cb_ref[:, cs] + cw_ref[0:1, cs] * am2 + cw_ref[1:2, cs] * am1 + cw_ref[2:3, cs] * a
        act = (_gelu(conv) * bgate).astype(BF16)
        acc = acc + _dot(act, wdn_ref[cs, :])
    if seq_mode:
        @pl.when(t_in == tiles_per_seq - 1)
        def _():
            conv_ref[0] = abuf[8 + tm - (CONV_W - 1):8 + tm, :]

        abuf[0:8, :] = abuf[tm:tm + 8, :]
    y_ref[...] = _rms(x1 + acc, nl_ref[...])


def _ffn(x2, yg, yn, wo, nf, wup, cw, cb, wdn, nl, prev, b, t, tm):
    n, d = x2.shape
    dff = wdn.shape[0]
    seq_mode = prev is None
    row = lambda w: pl.BlockSpec((tm, w), lambda i: (i, 0))
    in_specs = [row(d), row(yg.shape[1]), row(yn.shape[1]), _const_spec(wo.shape), _const_spec((1, d)),
                _const_spec(wup.shape), _const_spec(cw.shape), _const_spec((1, dff)), _const_spec(wdn.shape),
                _const_spec((1, d))]
    args = [x2, yg, yn, wo, nf, wup, cw, cb, wdn, nl]
    if seq_mode:
        tps = t // tm
        conv_shape = jax.ShapeDtypeStruct((b, CONV_W - 1, dff), F32)
        conv_spec = pl.BlockSpec((1, CONV_W - 1, dff), lambda i: (i // tps, 0, 0))
        scratch = [pltpu.VMEM((8 + tm, dff), F32)]
    else:
        tps = 1
        in_specs.append(row(2 * dff))
        args.append(prev)
        conv_shape = jax.ShapeDtypeStruct((n, 2 * dff), F32)
        conv_spec = row(2 * dff)
        scratch = []
    return pl.pallas_call(
        functools.partial(_ffn_kernel, seq_mode=seq_mode, tiles_per_seq=tps), grid=(n // tm,),
        in_specs=in_specs, out_specs=[row(d), conv_spec],
        out_shape=[jax.ShapeDtypeStruct((n, d), F32), conv_shape], scratch_shapes=scratch,
        compiler_params=pltpu.CompilerParams(dimension_semantics=("arbitrary",), vmem_limit_bytes=VMEM_LIMIT),
        name="ffn_seq" if seq_mode else "ffn_step")(*args)


def _pack_in_weights(w_in, w_gla_a, b_gla_a, b_nsa_gate):
    sizes = (GQ_W, GQ_W, GV_W, GLA_RANK, GV_W, NQ_W, KV_W, KV_W, KV_W, 3 * NSA_HEADS)
    offs = np.concatenate([[0], np.cumsum(sizes)])
    part = lambda i: w_in[:, offs[i]:offs[i + 1]]
    wm = jnp.concatenate([part(0), part(1), part(2), part(4), part(5), part(6), part(7), part(8)], axis=1).astype(BF16)
    small = jnp.concatenate([part(3), part(9)], axis=1)
    ws = jnp.pad(small, ((0, 0), (0, 128 - small.shape[1]))).astype(BF16)
    wa = jnp.pad(w_gla_a, ((0, 128 - GLA_RANK), (0, 0))).astype(BF16)
    ba = b_gla_a.reshape(1, GQ_W).astype(F32)
    bg = jnp.pad(b_nsa_gate.reshape(1, -1), ((0, 0), (GATE_OFF, 128 - GATE_OFF - 3 * NSA_HEADS))).astype(F32)
    return wm, ws, wa, ba, bg


def kernel(x_prompt, x_sample, cache_kv_cmp, cache_kv_sel, cache_kv_win, state_gla, state_conv, page_table,
           norm_mix, w_in, w_gla_a, b_gla_a, gla_norm, b_nsa_gate, w_out, norm_ffn, w_ffn_up, conv_w, conv_b,
           w_ffn_down, norm_final):
    bp, seq, d = x_prompt.shape
    db, dseq, _ = x_sample.shape
    assert dseq == 1 and norm_mix.shape[0] == 1
    npg = page_table.shape[1]
    psz = cache_kv_cmp.shape[2]
    past = npg * psz
    n_phys = cache_kv_cmp.shape[1]
    w_buf = cache_kv_win.shape[2]
    dff = w_ffn_down.shape[1]
    assert past % L_SEL == 0 and w_buf == WINDOW and past >= WINDOW and seq >= WINDOW + Q_BLOCK
    tm = min(512, seq)

    wm, ws, wa, ba, bg = _pack_in_weights(w_in[0], w_gla_a[0], b_gla_a[0], b_nsa_gate[0])
    nmix = norm_mix[0].reshape(1, d).astype(F32)
    wo = w_out[0].astype(BF16)
    nf = norm_ffn[0].reshape(1, d).astype(F32)
    wup = w_ffn_up[0].astype(BF16)
    cw = conv_w[0].astype(F32)
    cb = conv_b[0].reshape(1, dff).astype(F32)
    wdn = w_ffn_down[0].astype(BF16)
    nl = norm_final.reshape(1, d).astype(F32)

    (gq, gk, gv, gr, glog, kvc, kvs, kvw, qa, ksa, kwa, vst, vwt, cm, gates_t) = _proj(
        x_prompt, nmix, wm, ws, wa, ba, bg, tm, True)
    y_gla, s_p = _gla_prompt(gq, gk, gv, glog, gr, gla_norm[0], bp, seq, tm)
    cm_perm = jnp.concatenate([cm[:, 0::2], cm[:, 1::2]], axis=1)
    oc, selm, anyf = _nsa_cmp(qa, cm_perm)
    bits = _tile_bits(anyf, seq // L_SEL)
    y_nsa = _nsa_sel(bits, qa, ksa, kwa, vst, vwt, selm, oc, gates_t)
    y_p, conv_p = _ffn(x_prompt.reshape(bp * seq, d), y_gla, y_nsa, wo, nf, wup, cw, cb, wdn, nl, None, bp, seq, tm)
    w_keep = min(WINDOW, seq)

    def kv_rows(a_t):
        n_b, _, n_t = a_t.shape
        return a_t.reshape(1, n_b, 2, NSA_G, NSA_DH, n_t).transpose(0, 1, 5, 2, 3, 4)

    def token_minor(cache):
        return cache.transpose(0, 2, 3, 4, 1)

    out_prompt = (y_p.reshape(bp, seq, d), kv_rows(kvc), kv_rows(kvs), kv_rows(kvw[:, :, seq - w_keep:]),
                  s_p[None], conv_p[None])

    xs3 = x_sample.reshape(1, db, d)
    (gq, gk, gv, gr, glog, kvc_st, kvs_st, kvw_st, nqh, gates, kvs_s, kvw_s) = _proj(
        xs3, nmix, wm, ws, wa, ba, bg, db, False)
    y_gla_s, s_s = _gla_step(gq, gk, glog, gv, gr, gla_norm[0], state_gla[0].astype(F32))
    q3 = nqh[0].transpose(1, 0, 2)
    cm_s = _page_means(token_minor(cache_kv_cmp[0]), page_table)
    k_sel = min(N_SEL, past // L_SEL + 1) - 1
    oc_s, idx = _nsa_cmp_step(q3, cm_s, past, k_sel, bb=min(8, db))
    yn_s, win_out = _nsa_sel_step(page_table, idx, q3, oc_s, gates.reshape(db, 1, 128), kvs_s.reshape(db, 1, KV_W),
                                  kvw_s.reshape(db, 1, KV_W), kvw_st[0],
                                  token_minor(cache_kv_win[0]).reshape(db, KV_W, w_buf),
                                  token_minor(cache_kv_sel[0]), past, k_sel)
    y_s, conv_s = _ffn(x_sample.reshape(db, d), y_gla_s, yn_s.reshape(db, NQ_W), wo, nf, wup, cw, cb, wdn, nl,
                       state_conv[0].reshape(db, (CONV_W - 1) * dff).astype(F32), db, 1, db)
    sample_rows = lambda a_t: a_t[0].T.reshape(1, db, 1, 2, NSA_G, NSA_DH)
    out_sample = (y_s.reshape(db, 1, d), sample_rows(kvc_st), sample_rows(kvs_st), kv_rows(win_out),
                  s_s.reshape(1, db, GLA_HEADS, GLA_DK, GLA_DV), conv_s.reshape(1, db, CONV_W - 1, dff))
    return (out_prompt[0], out_sample[0]) + out_prompt[1:] + out_sample[1:]
```

```python
import functools

import numpy as np
import jax
import jax.numpy as jnp
from jax import lax
from jax.experimental import pallas as pl
from jax.experimental.pallas import tpu as pltpu

F32 = jnp.float32
BF16 = jnp.bfloat16

GLA_HEADS = 8
GLA_DK = 32
GLA_DV = 64
GLA_RANK = 16
GLA_TAU = 16.0
GLA_CHUNK = 64
NSA_HEADS = 8
NSA_G = 2
NSA_REP = 4
NSA_DH = 64
L_CMP = 32
L_SEL = 64
N_SEL = 16
WINDOW = 512
Q_BLOCK = 128
CONV_W = 3
EPS = 1e-6
NEG = -1e30
SLOPES = tuple(float(2.0 ** -(h + 1)) for h in range(NSA_HEADS))

GQ_W = GLA_HEADS * GLA_DK
GV_W = GLA_HEADS * GLA_DV
NQ_W = NSA_HEADS * NSA_DH
KV_W = 2 * NSA_G * NSA_DH
GATE_OFF = GLA_RANK
SEL_TILE = 256
VMEM_LIMIT = 56 * 1024 * 1024


def _nt(a, b):
    return lax.dot_general(a, b, (((1,), (1,)), ((), ())), preferred_element_type=F32)


def _tn(a, b):
    return lax.dot_general(a, b, (((0,), (0,)), ((), ())), preferred_element_type=F32)


def _dot(a, b):
    return jnp.dot(a, b, preferred_element_type=F32)


def _split3(x):
    hi = x.astype(BF16)
    r1 = x - hi.astype(F32)
    mid = r1.astype(BF16)
    lo = (r1 - mid.astype(F32)).astype(BF16)
    return hi, mid, lo


def _dot_exact_lhs(m_bf16, x):
    hi, mid, lo = _split3(x)
    return _dot(m_bf16, hi) + _dot(m_bf16, mid) + _dot(m_bf16, lo)


def _sigmoid(x):
    return 1.0 / (1.0 + jnp.exp(-x))


def _log_sigmoid(x):
    return jnp.minimum(x, 0.0) - jnp.log(1.0 + jnp.exp(-jnp.abs(x)))


def _rms(x, g):
    ms = jnp.mean(x * x, axis=-1, keepdims=True)
    return x * lax.rsqrt(ms + EPS) * g


def _const_spec(shape):
    nd = len(shape)
    return pl.BlockSpec(shape, lambda *_: (0,) * nd, pipeline_mode=pl.Buffered(1))


def _lane_halves(pair):
    return pair, pltpu.roll(pair, NSA_DH, 1)


def _proj_kernel(x_ref, g_ref, wm_ref, ws_ref, wa_ref, ba_ref, bg_ref,
                 gq_ref, gk_ref, gv_ref, gr_ref, glog_ref, kvc_ref, kvs_ref, kvw_ref, *attn_refs, prompt):
    tm = x_ref.shape[0]
    h = _rms(x_ref[...], g_ref[...]).astype(BF16)

    def mm(a, b):
        return _dot(h, wm_ref[:, a:b])

    o = 0
    gq_ref[...] = mm(o, o + GQ_W) * (GLA_DK ** -0.5); o += GQ_W
    gk_ref[...] = mm(o, o + GQ_W); o += GQ_W
    gv_ref[...] = mm(o, o + GV_W); o += GV_W
    gr_ref[...] = mm(o, o + GV_W); o += GV_W
    nq = mm(o, o + NQ_W) * (NSA_DH ** -0.5); o += NQ_W
    kvc = mm(o, o + KV_W); o += KV_W
    kvs = mm(o, o + KV_W); o += KV_W
    kvw = mm(o, o + KV_W); o += KV_W
    kvs_t = kvs.T
    kvw_t = kvw.T
    kvc_ref[0] = kvc.T
    kvs_ref[0] = kvs_t
    kvw_ref[0] = kvw_t
    small = _dot(h, ws_ref[...])
    z = _dot(small.astype(BF16), wa_ref[...]) + ba_ref[...]
    glog_ref[...] = _log_sigmoid(z) * (1.0 / GLA_TAU)
    gates = _sigmoid(small + bg_ref[...])
    if not prompt:
        nqh_ref, gates_ref, kvs_rows_ref, kvw_rows_ref = attn_refs
        for hh in range(NSA_HEADS):
            nqh_ref[0, hh] = nq[:, hh * NSA_DH:(hh + 1) * NSA_DH].astype(BF16)
        gates_ref[...] = gates
        kvs_rows_ref[...] = kvs
        kvw_rows_ref[...] = kvw
        return
    qa_ref, ksa_ref, kwa_ref, vst_ref, vwt_ref, cm_ref, gt_ref = attn_refs
    lane = lax.broadcasted_iota(jnp.int32, (tm, 128), 1)
    low = lane < NSA_DH
    for j in range(NSA_HEADS // 2):
        for hh, qh in zip((2 * j, 2 * j + 1), _lane_halves(nq[:, j * 128:(j + 1) * 128])):
            qa_ref[0, hh] = jnp.where(low, qh, jnp.where(lane == NSA_DH, SLOPES[hh], 0.0)).astype(BF16)
    pos = pl.program_id(1) * tm + lax.broadcasted_iota(jnp.int32, (tm, 128), 0)
    tile_pos = (pos % SEL_TILE).astype(F32)
    for src, src_t, ka_ref, vt_ref, tile in ((kvs, kvs_t, ksa_ref, vst_ref, SEL_TILE),
                                             (kvw, kvw_t, kwa_ref, vwt_ref, Q_BLOCK)):
        for g, kg in enumerate(_lane_halves(src[:, 0:NSA_G * NSA_DH])):
            ka_ref[0, g] = jnp.where(low, kg, tile_pos).astype(BF16)
        vt = src_t[NSA_G * NSA_DH:, :].astype(BF16)
        for g in range(NSA_G):
            for c in range(tm // tile):
                vt_ref[0, g, c] = vt[g * NSA_DH:(g + 1) * NSA_DH, c * tile:(c + 1) * tile]
    cm_ref[0] = kvc.reshape(tm // L_CMP, L_CMP, KV_W).sum(axis=1) * (1.0 / L_CMP)
    gt_ref[...] = gates.T


def _proj(x3, norm_g, wm, ws, wa, ba, bg, tm, prompt):
    b, t, d = x3.shape
    n = b * t
    nt = t // tm
    x2 = x3.reshape(n, d)
    row = lambda w: pl.BlockSpec((tm, w), lambda bi, ti: (bi * nt + ti, 0))
    sds = jax.ShapeDtypeStruct
    kvt = pl.BlockSpec((1, KV_W, tm), lambda bi, ti: (bi, 0, ti))
    out_shape = [sds((n, GQ_W), F32), sds((n, GQ_W), F32), sds((n, GV_W), F32), sds((n, GV_W), F32),
                 sds((n, GQ_W), F32), sds((b, KV_W, t), F32), sds((b, KV_W, t), F32), sds((b, KV_W, t), F32)]
    out_specs = [row(GQ_W), row(GQ_W), row(GV_W), row(GV_W), row(GQ_W), kvt, kvt, kvt]
    if prompt:
        hm = lambda nh: pl.BlockSpec((1, nh, tm, 128), lambda bi, ti: (bi, 0, ti, 0))
        vt = lambda tile: pl.BlockSpec((1, NSA_G, tm // tile, NSA_DH, tile), lambda bi, ti: (bi, 0, ti, 0, 0))
        out_shape += [sds((b, NSA_HEADS, t, 128), BF16), sds((b, NSA_G, t, 128), BF16), sds((b, NSA_G, t, 128), BF16),
                      sds((b, NSA_G, t // SEL_TILE, NSA_DH, SEL_TILE), BF16),
                      sds((b, NSA_G, t // Q_BLOCK, NSA_DH, Q_BLOCK), BF16),
                      sds((b, t // L_CMP, KV_W), F32), sds((128, n), F32)]
        out_specs += [hm(NSA_HEADS), hm(NSA_G), hm(NSA_G), vt(SEL_TILE), vt(Q_BLOCK),
                      pl.BlockSpec((1, tm // L_CMP, KV_W), lambda bi, ti: (bi, ti, 0)),
                      pl.BlockSpec((128, tm), lambda bi, ti: (0, bi * nt + ti))]
    else:
        out_shape += [sds((b, NSA_HEADS, t, NSA_DH), BF16), sds((n, 128), F32), sds((n, KV_W), F32), sds((n, KV_W), F32)]
        out_specs += [pl.BlockSpec((1, NSA_HEADS, tm, NSA_DH), lambda bi, ti: (bi, 0, ti, 0)), row(128),
                      row(KV_W), row(KV_W)]
    in_specs = [row(d), _const_spec((1, d)), _const_spec(wm.shape), _const_spec(ws.shape),
                _const_spec(wa.shape), _const_spec(ba.shape), _const_spec(bg.shape)]
    return pl.pallas_call(
        functools.partial(_proj_kernel, prompt=prompt), grid=(b, nt), in_specs=in_specs, out_specs=out_specs,
        out_shape=out_shape,
        compiler_params=pltpu.CompilerParams(dimension_semantics=("arbitrary", "arbitrary"),
                                             vmem_limit_bytes=VMEM_LIMIT),
        name="proj")(x2, norm_g, wm, ws, wa, ba, bg)


GLA_CHUNKS_PER_TRIP = 4


def _gla_consts():
    c = GLA_CHUNK
    t = np.arange(c)
    s = t[None, :]
    masks = []
    m = c // 2
    while m >= 1:
        blk = t // (2 * m)
        upper = t > blk * 2 * m + m - 1
        masks.append((upper[:, None] & (~upper)[None, :] & (blk[:, None] == blk[None, :])).astype(np.float32))
        m //= 2
    mcat = np.tril(np.ones((c, c), np.float32))
    lvl = np.stack([np.tile(mk, (1, GLA_HEADS)) for mk in masks])
    eye = np.tile(np.eye(c, dtype=np.float32), (1, GLA_HEADS))
    hk = np.arange(GLA_HEADS * GLA_DK) // GLA_DK
    hc = np.arange(GLA_HEADS * c) // c
    hv = np.arange(GLA_HEADS * GLA_DV) // GLA_DV
    bdk = (hc[:, None] == hk[None, :]).astype(np.float32)
    bdv = (hc[:, None] == hv[None, :]).astype(np.float32)
    bds = (hv[:, None] == hk[None, :]).astype(np.float32)
    hsum = (hk[:, None] == hc[None, :]).astype(np.float32)
    hmean = (hv[:, None] == hv[None, :]).astype(np.float32) / GLA_DV
    return (jnp.asarray(mcat, BF16), jnp.asarray(lvl, F32), jnp.asarray(eye, F32), jnp.asarray(bdk, BF16),
            jnp.asarray(bdv, BF16), jnp.asarray(bds, F32), jnp.asarray(hsum, BF16), jnp.asarray(hmean, BF16))


def _gla_kernel(q_ref, k_ref, v_ref, g_ref, r_ref, gain_ref, mcat_ref, lvl_ref, eye_ref, bdk_ref, bdv_ref,
                bds_ref, hsum_ref, hmean_ref, y_ref, st_ref, s_sc):
    c = GLA_CHUNK
    nlev = lvl_ref.shape[0]
    n_chunks = q_ref.shape[0] // c

    @pl.when(pl.program_id(1) == 0)
    def _():
        s_sc[...] = jnp.zeros_like(s_sc)

    t_row = lax.broadcasted_iota(jnp.int32, (c, 1), 0)
    halves = [c >> (lv + 1) for lv in range(nlev)]
    offs = [t_row - ((t_row // (2 * m)) * (2 * m) + m - 1) for m in halves]

    def ref_rows(b, lv):
        m = halves[lv]
        if 2 * m >= 16:
            return jnp.concatenate([jnp.broadcast_to(b[s0 + m - 1:s0 + m, :], (2 * m, b.shape[1]))
                                    for s0 in range(0, c, 2 * m)], axis=0)
        out = b
        for d in range(-(m - 1), m + 1):
            if d != 0:
                out = jnp.where(offs[lv] == d, pltpu.roll(b, d % c, 0), out)
        return out

    def chunk(ci):
        r0 = pl.multiple_of(ci * c, c)
        q = q_ref[pl.ds(r0, c), :]
        k = k_ref[pl.ds(r0, c), :]
        v = v_ref[pl.ds(r0, c), :].astype(BF16)
        g = g_ref[pl.ds(r0, c), :]
        b = _dot_exact_lhs(mcat_ref[...], g)
        a = _dot((q * k).astype(BF16), hsum_ref[...]) * eye_ref[...]
        for lv in range(nlev):
            delta = b - ref_rows(b, lv)
            upper = offs[lv] > 0
            qt = (q * jnp.exp(jnp.where(upper, delta, 0.0))).astype(BF16)
            kt = (k * jnp.exp(jnp.where(upper, 0.0, -delta))).astype(BF16)
            kbd = jnp.concatenate([kt] * GLA_HEADS, axis=0) * bdk_ref[...]
            a = a + _nt(qt, kbd) * lvl_ref[lv]
        vbd = jnp.concatenate([v] * GLA_HEADS, axis=0) * bdv_ref[...]
        s_t = s_sc[...]
        o = _dot(a.astype(BF16), vbd) + _nt((q * jnp.exp(b)).astype(BF16), s_t.astype(BF16))
        b_last = b[c - 1:c, :]
        kd = (k * jnp.exp(b_last - b)).astype(BF16)
        s_sc[...] = s_t * jnp.exp(b_last) + _tn(v, kd) * bds_ref[...]
        o2 = o * o
        o2h = o2.astype(BF16)
        o2l = (o2 - o2h.astype(F32)).astype(BF16)
        ms2 = _dot(jnp.concatenate([o2h, o2l], axis=0), hmean_ref[...])
        ms = ms2[0:c] + ms2[c:2 * c]
        r = r_ref[pl.ds(r0, c), :]
        y = o * lax.rsqrt(ms + EPS) * gain_ref[...] * (r * _sigmoid(r))
        y_ref[pl.ds(r0, c), :] = y.astype(BF16)

    def chunk_group(pi, carry):
        for j in range(GLA_CHUNKS_PER_TRIP):
            chunk(GLA_CHUNKS_PER_TRIP * pi + j)
        return carry

    lax.fori_loop(0, n_chunks // GLA_CHUNKS_PER_TRIP, chunk_group, 0)

    @pl.when(pl.program_id(1) == pl.num_programs(1) - 1)
    def _():
        st_ref[0] = s_sc[...]


def _gla_prompt(gq, gk, gv, glog, gr, gain, b, t, tc):
    n = b * t
    nt = t // tc
    consts = _gla_consts()
    gain_t = jnp.tile(gain.astype(F32).reshape(1, GLA_DV), (1, GLA_HEADS))
    row = lambda w: pl.BlockSpec((tc, w), lambda bi, ti: (bi * nt + ti, 0))
    in_specs = [row(GQ_W), row(GQ_W), row(GV_W), row(GQ_W), row(GV_W), _const_spec((1, GV_W))]
    in_specs += [_const_spec(a.shape) for a in consts]
    y, st = pl.pallas_call(
        _gla_kernel, grid=(b, nt), in_specs=in_specs,
        out_specs=[row(GV_W), pl.BlockSpec((1, GV_W, GQ_W), lambda bi, ti: (bi, 0, 0))],
        out_shape=[jax.ShapeDtypeStruct((n, GV_W), BF16), jax.ShapeDtypeStruct((b, GV_W, GQ_W), F32)],
        scratch_shapes=[pltpu.VMEM((GV_W, GQ_W), F32)],
        compiler_params=pltpu.CompilerParams(dimension_semantics=("arbitrary", "arbitrary"),
                                             vmem_limit_bytes=VMEM_LIMIT),
        name="gla_prompt")(gq, gk, gv, glog, gr, gain_t, *consts)
    s = jnp.stack([st[:, h * GLA_DV:(h + 1) * GLA_DV, h * GLA_DK:(h + 1) * GLA_DK] for h in range(GLA_HEADS)], axis=1)
    return y, s.transpose(0, 1, 3, 2)


def _gla_step_kernel(q_ref, k_ref, g_ref, v_ref, r_ref, gain_ref, s0_ref, y_ref, s1_ref):
    bb = q_ref.shape[0]
    hd = GLA_HEADS * GLA_DK
    ri = lax.broadcasted_iota(jnp.int32, (hd, hd), 0)
    ci = lax.broadcasted_iota(jnp.int32, (hd, hd), 1)
    eye = ri == ci
    rowhead = lax.broadcasted_iota(jnp.int32, (hd, GLA_DV), 0) // GLA_DK

    def col(x_row):
        return jnp.sum(jnp.where(eye, jnp.broadcast_to(x_row, (hd, hd)), 0.0), axis=1, keepdims=True)

    for i in range(bb):
        q = q_ref[i:i + 1, :]
        k = k_ref[i:i + 1, :]
        eg = jnp.exp(g_ref[i:i + 1, :])
        v3 = v_ref[i]
        s0 = s0_ref[i]
        vrows = jnp.zeros((hd, GLA_DV), F32)
        for h in range(GLA_HEADS):
            vrows = jnp.where(rowhead == h, jnp.broadcast_to(v3[h:h + 1, :], (hd, GLA_DV)), vrows)
        s1_ref[i] = col(eg) * s0 + col(k) * vrows
        t1 = (col(q * eg) * s0).reshape(GLA_HEADS, GLA_DK, GLA_DV).sum(axis=1)
        qk = col(q * k).reshape(GLA_HEADS, GLA_DK, 1).sum(axis=1)
        o = t1 + qk * v3
        ms = jnp.mean(o * o, axis=-1, keepdims=True)
        r3 = r_ref[i]
        y_ref[i] = (o * lax.rsqrt(ms + EPS) * gain_ref[...] * (r3 * _sigmoid(r3))).astype(BF16)


def _gla_step(gq, gk, glog, gv, gr, gain, s0, bb=8):
    n = gq.shape[0]
    hd = GLA_HEADS * GLA_DK
    v3 = gv.reshape(n, GLA_HEADS, GLA_DV)
    r3 = gr.reshape(n, GLA_HEADS, GLA_DV)
    s0r = s0.reshape(n, hd, GLA_DV)
    row = pl.BlockSpec((bb, hd), lambda i: (i, 0))
    h3 = pl.BlockSpec((bb, GLA_HEADS, GLA_DV), lambda i: (i, 0, 0))
    st = pl.BlockSpec((bb, hd, GLA_DV), lambda i: (i, 0, 0))
    y, s1 = pl.pallas_call(
        _gla_step_kernel, grid=(n // bb,),
        in_specs=[row, row, row, h3, h3, _const_spec((1, GLA_DV)), st],
        out_specs=[h3, st],
        out_shape=[jax.ShapeDtypeStruct((n, GLA_HEADS, GLA_DV), BF16), jax.ShapeDtypeStruct((n, hd, GLA_DV), F32)],
        compiler_params=pltpu.CompilerParams(dimension_semantics=("arbitrary",)),
        name="gla_step")(gq, gk, glog, v3, r3, gain.astype(F32).reshape(1, GLA_DV), s0r)
    return y.reshape(n, GV_W), s1


def _topk_mask_t(v, k_sel):
    n = v.shape[0]
    sb = lax.broadcasted_iota(jnp.int32, v.shape, 0)
    sel = jnp.zeros(v.shape, F32)
    for _ in range(k_sel):
        m = jnp.max(v, axis=0, keepdims=True)
        idx = jnp.min(jnp.where(v == m, sb, n), axis=0, keepdims=True)
        hit = sb == idx
        sel = jnp.where(hit, 1.0, sel)
        v = jnp.where(hit, -jnp.inf, v)
    return sel


def _topk_mask(v, k_sel):
    n = v.shape[-1]
    sb = lax.broadcasted_iota(jnp.int32, v.shape, 1)
    sel = jnp.zeros(v.shape, F32)
    idxs = []
    for _ in range(k_sel):
        m = jnp.max(v, axis=-1, keepdims=True)
        idx = jnp.min(jnp.where(v == m, sb, n), axis=-1, keepdims=True)
        hit = sb == idx
        sel = jnp.where(hit, 1.0, sel)
        v = jnp.where(hit, -jnp.inf, v)
        idxs.append(idx)
    return sel, idxs


def _group_queries(q_ref, g):
    return jnp.concatenate([q_ref[0, g * NSA_REP + r] for r in range(NSA_REP)], axis=0)


def _nsa_cmp_kernel(q_ref, cm_ref, oc_ref, selm_ref, any_ref):
    nc = cm_ref.shape[1]
    half = nc // 2
    ns = half
    q0 = pl.program_id(1) * Q_BLOCK
    qpos = q0 + lax.broadcasted_iota(jnp.int32, (1, Q_BLOCK), 1)
    row = lax.broadcasted_iota(jnp.int32, (nc, 1), 0)
    blk = jnp.where(row < half, 2 * row, 2 * (row - half) + 1)
    mask_c = ((blk + 1) * L_CMP - 1) <= qpos
    dist_c = qpos.astype(F32) - (blk.astype(F32) * L_CMP + (L_CMP - 1) / 2.0)
    sb = lax.broadcasted_iota(jnp.int32, (ns, 1), 0)
    cur = qpos // L_SEL
    forced = (sb == 0) | (sb == cur) | (sb == cur - 1)
    future = sb * L_SEL > qpos
    cm = cm_ref[0]
    low = lax.broadcasted_iota(jnp.int32, (nc, 128), 1) < NSA_DH
    vt_all = cm[:, NSA_G * NSA_DH:].T
    ones = jnp.ones((8, Q_BLOCK), BF16)
    for g, kg in enumerate(_lane_halves(cm[:, 0:NSA_G * NSA_DH])):
        kc = jnp.where(low, kg, 0.0).astype(BF16)
        vct = vt_all[g * NSA_DH:(g + 1) * NSA_DH, :].astype(BF16)
        s_all = _nt(kc, _group_queries(q_ref, g))
        imp = jnp.zeros((nc, Q_BLOCK), F32)
        ps = []
        for r in range(NSA_REP):
            h = g * NSA_REP + r
            s = s_all[:, r * Q_BLOCK:(r + 1) * Q_BLOCK] - SLOPES[h] * dist_c
            s = jnp.where(mask_c, s, NEG)
            e = jnp.exp(s - jnp.max(s, axis=0, keepdims=True))
            p = jnp.where(mask_c, e * (1.0 / jnp.sum(e, axis=0, keepdims=True)), 0.0)
            ps.append(p.astype(BF16))
            imp = imp + p
        oct = _dot(vct, jnp.concatenate(ps, axis=1))
        for r in range(NSA_REP):
            oc_ref[0, g * NSA_REP + r] = oct[:, r * Q_BLOCK:(r + 1) * Q_BLOCK]
        imp2 = imp[:half] + imp[half:]
        v = jnp.where(future, -jnp.inf, jnp.where(forced, jnp.inf, imp2))
        sel = _topk_mask_t(v, min(N_SEL, ns))
        selm_ref[0, g] = sel
        any_ref[0, g:g + 1, :] = _nt(ones, sel.astype(BF16))[0:1]


def _nsa_cmp(qa, cm_perm):
    b, _, t, _ = qa.shape
    nqb = t // Q_BLOCK
    nc = t // L_CMP
    ns = t // L_SEL
    return pl.pallas_call(
        _nsa_cmp_kernel, grid=(b, nqb),
        in_specs=[pl.BlockSpec((1, NSA_HEADS, Q_BLOCK, 128), lambda bi, i: (bi, 0, i, 0)),
                  pl.BlockSpec((1, nc, KV_W), lambda bi, i: (bi, 0, 0))],
        out_specs=[pl.BlockSpec((1, NSA_HEADS, NSA_DH, Q_BLOCK), lambda bi, i: (bi, 0, 0, i)),
                   pl.BlockSpec((1, NSA_G, ns, Q_BLOCK), lambda bi, i: (bi, 0, 0, i)),
                   pl.BlockSpec((1, NSA_G, ns), lambda bi, i: (bi * nqb + i, 0, 0))],
        out_shape=[jax.ShapeDtypeStruct((b, NSA_HEADS, NSA_DH, t), F32),
                   jax.ShapeDtypeStruct((b, NSA_G, ns, t), F32),
                   jax.ShapeDtypeStruct((b * nqb, NSA_G, ns), F32)],
        compiler_params=pltpu.CompilerParams(dimension_semantics=("arbitrary", "arbitrary"),
                                             vmem_limit_bytes=VMEM_LIMIT),
        name="nsa_cmp")(qa, cm_perm)


WIN_TILES = (WINDOW + Q_BLOCK) // Q_BLOCK


def _nsa_sel_kernel(bits_ref, q_ref, ksa_ref, kwa_ref, vst_ref, vwt_ref, selm_ref, oc_ref, gt_ref, y_ref,
                    m_sc, l_sc, acc_sc):
    t_len = ksa_ref.shape[2]
    nqb = pl.num_programs(1)
    bi = pl.program_id(0)
    i = pl.program_id(1)
    q0 = i * Q_BLOCK
    qpos = q0 + lax.broadcasted_iota(jnp.int32, (1, Q_BLOCK), 1)
    diag = q0 // SEL_TILE
    nwords = (t_len // SEL_TILE + 15) // 16
    per = SEL_TILE // L_SEL
    wt0 = jnp.clip(i - WINDOW // Q_BLOCK, 0, t_len // Q_BLOCK - WIN_TILES)
    wstart = pl.multiple_of(wt0 * Q_BLOCK, Q_BLOCK)
    wkeys = WIN_TILES * Q_BLOCK
    wdist = qpos - (wstart + lax.broadcasted_iota(jnp.int32, (wkeys, 1), 0))
    wvalid = (wdist >= 0) & (wdist <= WINDOW)
    kiota = lax.broadcasted_iota(jnp.int32, (SEL_TILE, 1), 0)
    gts = gt_ref[...]
    q_alls = [_group_queries(q_ref, g) for g in range(NSA_G)]
    m_sc[...] = jnp.full(m_sc.shape, NEG, F32)
    l_sc[...] = jnp.zeros_like(l_sc)
    acc_sc[...] = jnp.zeros_like(acc_sc)
    base = (bi * nqb + i) * NSA_G * nwords

    def attend_tile(g, t, causal):
        ks = pl.multiple_of(t * SEL_TILE, SEL_TILE)
        s_all = _nt(ksa_ref[0, g, pl.ds(ks, SEL_TILE), :], q_alls[g])
        mk = jnp.concatenate([jnp.broadcast_to(selm_ref[0, g, pl.ds(t * per + j, 1), :], (L_SEL, Q_BLOCK))
                              for j in range(per)], axis=0) > 0.5
        if causal:
            mk = mk & ((ks + kiota) <= qpos)
        shift = (ks - q0).astype(F32)
        ps, alphas = [], []
        for r in range(NSA_REP):
            cs = slice(r * Q_BLOCK, (r + 1) * Q_BLOCK)
            s = jnp.where(mk, s_all[:, cs] + SLOPES[g * NSA_REP + r] * shift, NEG)
            m_old = m_sc[g, :, cs]
            m_new = jnp.maximum(m_old, jnp.max(s, axis=0, keepdims=True))
            alpha = jnp.exp(m_old - m_new)
            p = jnp.exp(s - m_new)
            l_sc[g, :, cs] = alpha * l_sc[g, :, cs] + jnp.sum(p, axis=0, keepdims=True)
            m_sc[g, :, cs] = m_new
            ps.append(p.astype(BF16))
            alphas.append(alpha)
        acc_sc[g] = jnp.concatenate(alphas, axis=1) * acc_sc[g] + _dot(vst_ref[0, g, t], jnp.concatenate(ps, axis=1))

    def tile(t, carry):
        flag = bits_ref[base + t // 16]
        for g in range(1, NSA_G):
            flag = flag | bits_ref[base + g * nwords + t // 16]

        @pl.when((lax.shift_right_logical(flag, t % 16) & 1) == 1)
        def _():
            for g in range(NSA_G):
                attend_tile(g, t, False)

        return carry

    lax.fori_loop(0, diag, tile, 0)
    for g in range(NSA_G):
        attend_tile(g, diag, True)

    for g in range(NSA_G):
        q_all = q_alls[g]
        o_s = acc_sc[g] * (1.0 / l_sc[g])
        s_all = _nt(kwa_ref[0, g, pl.ds(wstart, wkeys), :], q_all)
        vw = jnp.concatenate([vwt_ref[0, g, wt0 + j] for j in range(WIN_TILES)], axis=1)
        shifts = [(((wstart + j * Q_BLOCK) // SEL_TILE) * SEL_TILE - q0).astype(F32) for j in range(WIN_TILES)]
        ps = []
        for r in range(NSA_REP):
            slope = SLOPES[g * NSA_REP + r]
            s = jnp.concatenate([s_all[j * Q_BLOCK:(j + 1) * Q_BLOCK, r * Q_BLOCK:(r + 1) * Q_BLOCK] + slope * shifts[j]
                                 for j in range(WIN_TILES)], axis=0)
            s = jnp.where(wvalid, s, NEG)
            e = jnp.exp(s - jnp.max(s, axis=0, keepdims=True))
            ps.append((e * (1.0 / jnp.sum(e, axis=0, keepdims=True))).astype(BF16))
        o_w = _dot(vw, jnp.concatenate(ps, axis=1))

        outs = []
        for r in range(NSA_REP):
            h = g * NSA_REP + r
            c0 = GATE_OFF + h * 3
            cs = slice(r * Q_BLOCK, (r + 1) * Q_BLOCK)
            outs.append(gts[c0:c0 + 1, :] * oc_ref[0, h] + gts[c0 + 1:c0 + 2, :] * o_s[:, cs]
                        + gts[c0 + 2:c0 + 3, :] * o_w[:, cs])
        gw = NSA_REP * NSA_DH
        y_ref[:, g * gw:(g + 1) * gw] = jnp.concatenate(outs, axis=0).T.astype(BF16)


def _nsa_sel(bits, qa, ksa, kwa, vst, vwt, selm, oc, gates_t):
    b, _, t, _ = qa.shape
    nqb = t // Q_BLOCK
    ns = t // L_SEL
    whole = lambda a: pl.BlockSpec((1,) + a.shape[1:], lambda bi, i, *_: (bi,) + (0,) * (a.ndim - 1),
                                   pipeline_mode=pl.Buffered(1))
    cols = lambda lead: pl.BlockSpec((1,) + lead + (Q_BLOCK,), lambda bi, i, *_: (bi,) + (0,) * len(lead) + (i,))
    grid_spec = pltpu.PrefetchScalarGridSpec(
        num_scalar_prefetch=1, grid=(b, nqb),
        in_specs=[pl.BlockSpec((1, NSA_HEADS, Q_BLOCK, 128), lambda bi, i, *_: (bi, 0, i, 0)),
                  whole(ksa), whole(kwa), whole(vst), whole(vwt),
                  cols((NSA_G, ns)), cols((NSA_HEADS, NSA_DH)),
                  pl.BlockSpec((128, Q_BLOCK), lambda bi, i, *_: (0, bi * nqb + i))],
        out_specs=[pl.BlockSpec((Q_BLOCK, NQ_W), lambda bi, i, *_: (bi * nqb + i, 0))],
        scratch_shapes=[pltpu.VMEM((NSA_G, 1, NSA_REP * Q_BLOCK), F32), pltpu.VMEM((NSA_G, 1, NSA_REP * Q_BLOCK), F32),
                        pltpu.VMEM((NSA_G, NSA_DH, NSA_REP * Q_BLOCK), F32)])
    return pl.pallas_call(
        _nsa_sel_kernel, grid_spec=grid_spec,
        out_shape=[jax.ShapeDtypeStruct((b * t, NQ_W), BF16)],
        compiler_params=pltpu.CompilerParams(dimension_semantics=("arbitrary", "arbitrary"),
                                             vmem_limit_bytes=VMEM_LIMIT),
        name="nsa_sel")(bits, qa, ksa, kwa, vst, vwt, selm, oc, gates_t)[0]


def _tile_bits(anyf, ns):
    r = anyf.shape[0]
    per = SEL_TILE // L_SEL
    ntile = ns // per
    nwords = (ntile + 15) // 16
    tf = (anyf.reshape(r, NSA_G, ntile, per).max(axis=-1) > 0.5).astype(jnp.int32)
    tf = jnp.pad(tf, ((0, 0), (0, 0), (0, nwords * 16 - ntile))).reshape(r, NSA_G, nwords, 16)
    return jnp.sum(tf << jnp.arange(16, dtype=jnp.int32), axis=-1).reshape(-1).astype(jnp.int32)


PAGES_PER_STEP = 8


def _page_mean_kernel(pt_ref, *refs, steps_per_tile):
    avg_ref, o_ref = refs[-2:]
    j = pl.program_id(1)
    x = jnp.concatenate([refs[p][0].reshape(KV_W, refs[p].shape[-1]) for p in range(PAGES_PER_STEP)], axis=1)
    part = _dot(x.astype(BF16), avg_ref[j % steps_per_tile])

    @pl.when(j % steps_per_tile == 0)
    def _():
        o_ref[0] = part

    @pl.when(j % steps_per_tile != 0)
    def _():
        o_ref[0] += part


def _page_means(cache_t, page_table):
    db, npg = page_table.shape
    psz = cache_t.shape[-1]
    per = psz // L_CMP
    nblk = npg * per
    cols = min(128, nblk)
    step_blocks = PAGES_PER_STEP * per
    steps_per_tile = cols // step_blocks
    nj = npg // PAGES_PER_STEP
    tok = np.arange(PAGES_PER_STEP * psz)
    avg = np.zeros((steps_per_tile, PAGES_PER_STEP * psz, cols), np.float32)
    for s in range(steps_per_tile):
        avg[s, tok, s * step_blocks + tok // L_CMP] = 1.0 / L_CMP

    def spec(p):
        return pl.BlockSpec((1,) + cache_t.shape[1:],
                            lambda bi, j, pt: (pt[bi * npg + j * PAGES_PER_STEP + p], 0, 0, 0, 0))

    grid_spec = pltpu.PrefetchScalarGridSpec(
        num_scalar_prefetch=1, grid=(db, nj),
        in_specs=[spec(p) for p in range(PAGES_PER_STEP)]
        + [pl.BlockSpec(avg.shape, lambda bi, j, pt: (0, 0, 0), pipeline_mode=pl.Buffered(1))],
        out_specs=[pl.BlockSpec((1, KV_W, cols), lambda bi, j, pt: (bi, 0, j // steps_per_tile))])
    return pl.pallas_call(
        functools.partial(_page_mean_kernel, steps_per_tile=steps_per_tile), grid_spec=grid_spec,
        out_shape=[jax.ShapeDtypeStruct((db, KV_W, nblk), F32)],
        compiler_params=pltpu.CompilerParams(dimension_semantics=("arbitrary", "arbitrary")),
        name="page_means")(page_table.reshape(-1), *([cache_t] * PAGES_PER_STEP), jnp.asarray(avg, BF16))[0]


def _slope_col():
    row = lax.broadcasted_iota(jnp.int32, (NSA_HEADS, 1), 0)
    sl = jnp.zeros((NSA_HEADS, 1), F32)
    for h in range(NSA_HEADS):
        sl = jnp.where(row == h, SLOPES[h], sl)
    return row, sl


def _nsa_cmp_step_kernel(q_ref, cm_ref, oc_ref, idx_ref, *, past, k_sel):
    bb = q_ref.shape[0]
    nc = cm_ref.shape[2]
    half = nc // 2
    blk = lax.broadcasted_iota(jnp.int32, (1, nc), 1)
    pair = jnp.where(lax.broadcasted_iota(jnp.int32, (nc, half), 0) // (L_SEL // L_CMP)
                     == lax.broadcasted_iota(jnp.int32, (nc, half), 1), 1.0, 0.0).astype(BF16)
    mask_c = ((blk + 1) * L_CMP - 1) <= past
    dist_c = float(past) - (blk.astype(F32) * L_CMP + (L_CMP - 1) / 2.0)
    row, sl = _slope_col()
    imps = []
    for i in range(bb):
        q8 = q_ref[i]
        oc = jnp.zeros((NSA_HEADS, NSA_DH), F32)
        for g in range(NSA_G):
            kct = cm_ref[i, g * NSA_DH:(g + 1) * NSA_DH, :].astype(BF16)
            vct = cm_ref[i, (NSA_G + g) * NSA_DH:(NSA_G + g + 1) * NSA_DH, :].astype(BF16)
            s = _dot(q8, kct) - sl * dist_c
            s = jnp.where(mask_c, s, NEG)
            e = jnp.exp(s - jnp.max(s, axis=-1, keepdims=True))
            p = jnp.where(mask_c, e / jnp.sum(e, axis=-1, keepdims=True), 0.0)
            ingroup = (row // NSA_REP) == g
            oc = jnp.where(ingroup, _nt(p.astype(BF16), vct), oc)
            hi, mid, lo = _split3(jnp.where(ingroup, p, 0.0))
            imps.append(jnp.sum(_dot(hi, pair) + _dot(mid, pair) + _dot(lo, pair), axis=0, keepdims=True))
        oc_ref[i] = oc
    imp = jnp.concatenate(imps, axis=0)
    sb = lax.broadcasted_iota(jnp.int32, imp.shape, 1)
    forced = (sb == 0) | (sb == half - 1)
    _, idxs = _topk_mask(jnp.where(forced, jnp.inf, imp), k_sel)
    lane = lax.broadcasted_iota(jnp.int32, (bb * NSA_G, k_sel), 1)
    acc = jnp.zeros((bb * NSA_G, k_sel), jnp.int32)
    for t, ix in enumerate(idxs):
        acc = jnp.where(lane == t, ix, acc)
    for i in range(bb):
        idx_ref[i] = acc[i * NSA_G:(i + 1) * NSA_G, :]


def _nsa_cmp_step(q3, cm_s, past, k_sel, bb=8):
    db = q3.shape[0]
    nc = cm_s.shape[2]
    return pl.pallas_call(
        functools.partial(_nsa_cmp_step_kernel, past=past, k_sel=k_sel), grid=(db // bb,),
        in_specs=[pl.BlockSpec((bb, NSA_HEADS, NSA_DH), lambda bi: (bi, 0, 0)),
                  pl.BlockSpec((bb, KV_W, nc), lambda bi: (bi, 0, 0))],
        out_specs=[pl.BlockSpec((bb, NSA_HEADS, NSA_DH), lambda bi: (bi, 0, 0)),
                   pl.BlockSpec((bb, NSA_G, k_sel), lambda bi: (bi, 0, 0))],
        out_shape=[jax.ShapeDtypeStruct((db, NSA_HEADS, NSA_DH), F32),
                   jax.ShapeDtypeStruct((db, NSA_G, k_sel), jnp.int32)],
        compiler_params=pltpu.CompilerParams(dimension_semantics=("arbitrary",), vmem_limit_bytes=VMEM_LIMIT),
        name="nsa_cmp_step")(q3, cm_s)


def _nsa_sel_step_kernel(pt_ref, ix_ref, q_ref, oc_ref, gates_ref, ks_new_ref, kw_new_ref, kwt_ref, win_ref, cache_ref,
                         y_ref, wout_ref, blocks, sems, *, past, k_sel, npg, per_page):
    bi = pl.program_id(0)
    w_len = win_ref.shape[2]
    psz = cache_ref.shape[-1]
    n_sel = NSA_G * k_sel

    def page_copy(a, j):
        n = ix_ref[bi * n_sel + j]
        page = pt_ref[bi * npg + n // per_page]
        return pltpu.make_async_copy(cache_ref.at[page, a, j // k_sel], blocks.at[a, j], sems.at[a, j])

    for a in range(2):
        for j in range(n_sel):
            page_copy(a, j).start()
    row, sl = _slope_col()
    q8 = q_ref[0]
    q8f = q8.astype(F32)
    gts = gates_ref[0]
    gcol = []
    for c in range(3):
        col = jnp.zeros((NSA_HEADS, 1), F32)
        for h in range(NSA_HEADS):
            c0 = GATE_OFF + h * 3 + c
            col = jnp.where(row == h, gts[:, c0:c0 + 1], col)
        gcol.append(col)
    win = win_ref[0]
    ks_new = ks_new_ref[0]
    kw_new = kw_new_ref[0]
    wpos = (past - w_len) + lax.broadcasted_iota(jnp.int32, (1, w_len), 1)
    wdist = past - wpos
    wvalid = (wdist <= WINDOW) & (wpos >= 0)

    def joint_attend(s, valid, k_new, v_new, v_t):
        s_new = jnp.sum(q8f * k_new.astype(BF16).astype(F32), axis=-1, keepdims=True)
        s = jnp.where(valid, s, NEG)
        m = jnp.maximum(jnp.max(s, axis=-1, keepdims=True), s_new)
        e = jnp.where(valid, jnp.exp(s - m), 0.0)
        en = jnp.exp(s_new - m)
        l = jnp.sum(e, axis=-1, keepdims=True) + en
        return _nt((e / l).astype(BF16), v_t) + (en / l).astype(BF16).astype(F32) * v_new.astype(BF16).astype(F32)

    ksl = [slice(g * NSA_DH, (g + 1) * NSA_DH) for g in range(NSA_G)]
    vsl = [slice((NSA_G + g) * NSA_DH, (NSA_G + g + 1) * NSA_DH) for g in range(NSA_G)]
    o_w = []
    for g in range(NSA_G):
        s = _dot(q8, win[ksl[g], :].astype(BF16)) - sl * wdist.astype(F32)
        o_w.append(joint_attend(s, wvalid, kw_new[:, ksl[g]], kw_new[:, vsl[g]], win[vsl[g], :].astype(BF16)))
    kwt = kwt_ref[...]
    new_col = jnp.sum(jnp.where(lax.broadcasted_iota(jnp.int32, kwt.shape, 1) == bi, kwt, 0.0), axis=1, keepdims=True)
    wlane = lax.broadcasted_iota(jnp.int32, (1, w_len), 1)
    wout_ref[0] = jnp.where(wlane == w_len - 1, new_col, pltpu.roll(win, w_len - 1, 1))
    for a in range(2):
        for j in range(n_sel):
            page_copy(a, j).wait()
    lane = lax.broadcasted_iota(jnp.int32, (1, k_sel * psz), 1)
    within = lane % psz
    o = jnp.zeros((NSA_HEADS, NSA_DH), F32)
    for g in range(NSA_G):
        k_t = jnp.concatenate([blocks[0, g * k_sel + t] for t in range(k_sel)], axis=1).astype(BF16)
        v_t = jnp.concatenate([blocks[1, g * k_sel + t] for t in range(k_sel)], axis=1).astype(BF16)
        page_pos = jnp.zeros_like(lane)
        half = jnp.zeros_like(lane)
        for t in range(k_sel):
            n = ix_ref[bi * n_sel + g * k_sel + t]
            here = lane // psz == t
            page_pos = jnp.where(here, (n // per_page) * psz, page_pos)
            half = jnp.where(here, n % per_page, half)
        dist = past - (page_pos + within)
        valid = (within // L_SEL == half) & (dist >= 0)
        s = _dot(q8, k_t) - sl * dist.astype(F32)
        o_s = joint_attend(s, valid, ks_new[:, ksl[g]], ks_new[:, vsl[g]], v_t)
        og = gcol[0] * oc_ref[0] + gcol[1] * o_s + gcol[2] * o_w[g]
        o = jnp.where((row // NSA_REP) == g, og, o)
    y_ref[0] = o.astype(BF16)


def _nsa_sel_step(page_table, idx, q3, oc, gates3, ks_new3, kw_new3, kw_new_t, win_t, cache_t, past, k_sel):
    db, npg = page_table.shape
    w_len = win_t.shape[2]
    psz = cache_t.shape[-1]
    per_page = psz // L_SEL

    b3 = lambda w: pl.BlockSpec((1, 1, w), lambda bi, *_: (bi, 0, 0))
    h3 = pl.BlockSpec((1, NSA_HEADS, NSA_DH), lambda bi, *_: (bi, 0, 0))
    wsp = pl.BlockSpec((1, KV_W, w_len), lambda bi, *_: (bi, 0, 0))
    grid_spec = pltpu.PrefetchScalarGridSpec(
        num_scalar_prefetch=2, grid=(db,),
        in_specs=[h3, h3, b3(128), b3(KV_W), b3(KV_W),
                  pl.BlockSpec(kw_new_t.shape, lambda bi, *_: (0, 0), pipeline_mode=pl.Buffered(1)),
                  wsp, pl.BlockSpec(memory_space=pl.ANY)],
        out_specs=[h3, wsp],
        scratch_shapes=[pltpu.VMEM((2, NSA_G * k_sel, NSA_DH, psz), F32),
                        pltpu.SemaphoreType.DMA((2, NSA_G * k_sel))])
    return pl.pallas_call(
        functools.partial(_nsa_sel_step_kernel, past=past, k_sel=k_sel, npg=npg, per_page=per_page),
        grid_spec=grid_spec,
        out_shape=[jax.ShapeDtypeStruct((db, NSA_HEADS, NSA_DH), BF16), jax.ShapeDtypeStruct((db, KV_W, w_len), F32)],
        compiler_params=pltpu.CompilerParams(dimension_semantics=("arbitrary",), vmem_limit_bytes=VMEM_LIMIT),
        name="nsa_sel_step")(page_table.reshape(-1), idx.reshape(-1), q3, oc, gates3, ks_new3, kw_new3, kw_new_t,
                             win_t, cache_t)


FFN_CHUNK = 256


def _gelu(x):
    return 0.5 * x * (1.0 + lax.erf(x * (2.0 ** -0.5)))


def _ffn_kernel(*refs, seq_mode, tiles_per_seq):
    if seq_mode:
        (x_ref, yg_ref, yn_ref, wo_ref, nf_ref, wup_ref, cw_ref, cb_ref, wdn_ref, nl_ref,
         y_ref, conv_ref, abuf) = refs
    else:
        (x_ref, yg_ref, yn_ref, wo_ref, nf_ref, wup_ref, cw_ref, cb_ref, wdn_ref, nl_ref, prev_ref,
         y_ref, conv_ref) = refs
    tm = x_ref.shape[0]
    dff = wdn_ref.shape[0]
    gw = yg_ref.shape[1]
    x1 = x_ref[...] + _dot(yg_ref[...], wo_ref[0:gw, :]) + _dot(yn_ref[...], wo_ref[gw:, :])
    h2 = _rms(x1, nf_ref[...]).astype(BF16)
    if seq_mode:
        t_in = pl.program_id(0) % tiles_per_seq

        @pl.when(t_in == 0)
        def _():
            abuf[0:8, :] = jnp.zeros((8, dff), F32)

    acc = jnp.zeros(x1.shape, F32)
    for c0 in range(0, dff, FFN_CHUNK):
        cs = slice(c0, c0 + FFN_CHUNK)
        a = _dot(h2, wup_ref[:, cs])
        bgate = _dot(h2, wup_ref[:, dff + c0:dff + c0 + FFN_CHUNK])
        if seq_mode:
            abuf[8:8 + tm, cs] = a
            am1 = abuf[7:7 + tm, cs]
            am2 = abuf[6:6 + tm, cs]
        else:
            am2 = prev_ref[:, cs]
            am1 = prev_ref[:, dff + c0:dff + c0 + FFN_CHUNK]
            conv_ref[:, cs] = am1
            conv_ref[:, dff + c0:dff + c0 + FFN_CHUNK] = a
        conv = cb_ref[:, cs] + cw_ref[0:1, cs] * am2 + cw_ref[1:2, cs] * am1 + cw_ref[2:3, cs] * a
        act = (_gelu(conv) * bgate).astype(BF16)
        acc = acc + _dot(act, wdn_ref[cs, :])
    if seq_mode:
        @pl.when(t_in == tiles_per_seq - 1)
        def _():
            conv_ref[0] = abuf[8 + tm - (CONV_W - 1):8 + tm, :]

        abuf[0:8, :] = abuf[tm:tm + 8, :]
    y_ref[...] = _rms(x1 + acc, nl_ref[...])


def _ffn(x2, yg, yn, wo, nf, wup, cw, cb, wdn, nl, prev, b, t, tm):
    n, d = x2.shape
    dff = wdn.shape[0]
    seq_mode = prev is None
    row = lambda w: pl.BlockSpec((tm, w), lambda i: (i, 0))
    in_specs = [row(d), row(yg.shape[1]), row(yn.shape[1]), _const_spec(wo.shape), _const_spec((1, d)),
                _const_spec(wup.shape), _const_spec(cw.shape), _const_spec((1, dff)), _const_spec(wdn.shape),
                _const_spec((1, d))]
    args = [x2, yg, yn, wo, nf, wup, cw, cb, wdn, nl]
    if seq_mode:
        tps = t // tm
        conv_shape = jax.ShapeDtypeStruct((b, CONV_W - 1, dff), F32)
        conv_spec = pl.BlockSpec((1, CONV_W - 1, dff), lambda i: (i // tps, 0, 0))
        scratch = [pltpu.VMEM((8 + tm, dff), F32)]
    else:
        tps = 1
        in_specs.append(row(2 * dff))
        args.append(prev)
        conv_shape = jax.ShapeDtypeStruct((n, 2 * dff), F32)
        conv_spec = row(2 * dff)
        scratch = []
    return pl.pallas_call(
        functools.partial(_ffn_kernel, seq_mode=seq_mode, tiles_per_seq=tps), grid=(n // tm,),
        in_specs=in_specs, out_specs=[row(d), conv_spec],
        out_shape=[jax.ShapeDtypeStruct((n, d), F32), conv_shape], scratch_shapes=scratch,
        compiler_params=pltpu.CompilerParams(dimension_semantics=("arbitrary",), vmem_limit_bytes=VMEM_LIMIT),
        name="ffn_seq" if seq_mode else "ffn_step")(*args)


def _pack_in_weights(w_in, w_gla_a, b_gla_a, b_nsa_gate):
    sizes = (GQ_W, GQ_W, GV_W, GLA_RANK, GV_W, NQ_W, KV_W, KV_W, KV_W, 3 * NSA_HEADS)
    offs = np.concatenate([[0], np.cumsum(sizes)])
    part = lambda i: w_in[:, offs[i]:offs[i + 1]]
    wm = jnp.concatenate([part(0), part(1), part(2), part(4), part(5), part(6), part(7), part(8)], axis=1).astype(BF16)
    small = jnp.concatenate([part(3), part(9)], axis=1)
    ws = jnp.pad(small, ((0, 0), (0, 128 - small.shape[1]))).astype(BF16)
    wa = jnp.pad(w_gla_a, ((0, 128 - GLA_RANK), (0, 0))).astype(BF16)
    ba = b_gla_a.reshape(1, GQ_W).astype(F32)
    bg = jnp.pad(b_nsa_gate.reshape(1, -1), ((0, 0), (GATE_OFF, 128 - GATE_OFF - 3 * NSA_HEADS))).astype(F32)
    return wm, ws, wa, ba, bg


def kernel(x_prompt, x_sample, cache_kv_cmp, cache_kv_sel, cache_kv_win, state_gla, state_conv, page_table,
           norm_mix, w_in, w_gla_a, b_gla_a, gla_norm, b_nsa_gate, w_out, norm_ffn, w_ffn_up, conv_w, conv_b,
           w_ffn_down, norm_final):
    bp, seq, d = x_prompt.shape
    db, dseq, _ = x_sample.shape
    assert dseq == 1 and norm_mix.shape[0] == 1
    npg = page_table.shape[1]
    psz = cache_kv_cmp.shape[2]
    past = npg * psz
    n_phys = cache_kv_cmp.shape[1]
    w_buf = cache_kv_win.shape[2]
    dff = w_ffn_down.shape[1]
    assert past % L_SEL == 0 and w_buf == WINDOW and past >= WINDOW and seq >= WINDOW + Q_BLOCK
    tm = min(512, seq)

    wm, ws, wa, ba, bg = _pack_in_weights(w_in[0], w_gla_a[0], b_gla_a[0], b_nsa_gate[0])
    nmix = norm_mix[0].reshape(1, d).astype(F32)
    wo = w_out[0].astype(BF16)
    nf = norm_ffn[0].reshape(1, d).astype(F32)
    wup = w_ffn_up[0].astype(BF16)
    cw = conv_w[0].astype(F32)
    cb = conv_b[0].reshape(1, dff).astype(F32)
    wdn = w_ffn_down[0].astype(BF16)
    nl = norm_final.reshape(1, d).astype(F32)

    (gq, gk, gv, gr, glog, kvc, kvs, kvw, qa, ksa, kwa, vst, vwt, cm, gates_t) = _proj(
        x_prompt, nmix, wm, ws, wa, ba, bg, tm, True)
    y_gla, s_p = _gla_prompt(gq, gk, gv, glog, gr, gla_norm[0], bp, seq, tm)
    cm_perm = jnp.concatenate([cm[:, 0::2], cm[:, 1::2]], axis=1)
    oc, selm, anyf = _nsa_cmp(qa, cm_perm)
    bits = _tile_bits(anyf, seq // L_SEL)
    y_nsa = _nsa_sel(bits, qa, ksa, kwa, vst, vwt, selm, oc, gates_t)
    y_p, conv_p = _ffn(x_prompt.reshape(bp * seq, d), y_gla, y_nsa, wo, nf, wup, cw, cb, wdn, nl, None, bp, seq, tm)
    w_keep = min(WINDOW, seq)

    def kv_rows(a_t):
        n_b, _, n_t = a_t.shape
        return a_t.reshape(1, n_b, 2, NSA_G, NSA_DH, n_t).transpose(0, 1, 5, 2, 3, 4)

    def token_minor(cache):
        return cache.transpose(0, 2, 3, 4, 1)

    out_prompt = (y_p.reshape(bp, seq, d), kv_rows(kvc), kv_rows(kvs), kv_rows(kvw[:, :, seq - w_keep:]),
                  s_p[None], conv_p[None])

    xs3 = x_sample.reshape(1, db, d)
    (gq, gk, gv, gr, glog, kvc_st, kvs_st, kvw_st, nqh, gates, kvs_s, kvw_s) = _proj(
        xs3, nmix, wm, ws, wa, ba, bg, db, False)
    y_gla_s, s_s = _gla_step(gq, gk, glog, gv, gr, gla_norm[0], state_gla[0].astype(F32))
    q3 = nqh[0].transpose(1, 0, 2)
    cm_s = _page_means(token_minor(cache_kv_cmp[0]), page_table)
    k_sel = min(N_SEL, past // L_SEL + 1) - 1
    oc_s, idx = _nsa_cmp_step(q3, cm_s, past, k_sel, bb=min(8, db))
    yn_s, win_out = _nsa_sel_step(page_table, idx, q3, oc_s, gates.reshape(db, 1, 128), kvs_s.reshape(db, 1, KV_W),
                                  kvw_s.reshape(db, 1, KV_W), kvw_st[0],
                                  token_minor(cache_kv_win[0]).reshape(db, KV_W, w_buf),
                                  token_minor(cache_kv_sel[0]), past, k_sel)
    y_s, conv_s = _ffn(x_sample.reshape(db, d), y_gla_s, yn_s.reshape(db, NQ_W), wo, nf, wup, cw, cb, wdn, nl,
                       state_conv[0].reshape(db, (CONV_W - 1) * dff).astype(F32), db, 1, db)
    sample_rows = lambda a_t: a_t[0].T.reshape(1, db, 1, 2, NSA_G, NSA_DH)
    out_sample = (y_s.reshape(db, 1, d), sample_rows(kvc_st), sample_rows(kvs_st), kv_rows(win_out),
                  s_s.reshape(1, db, GLA_HEADS, GLA_DK, GLA_DV), conv_s.reshape(1, db, CONV_W - 1, dff))
    return (out_prompt[0], out_sample[0]) + out_prompt[1:] + out_sample[1:]
```

```python
import functools

import numpy as np
import jax
import jax.numpy as jnp
from jax import lax
from jax.experimental import pallas as pl
from jax.experimental.pallas import tpu as pltpu

F32 = jnp.float32
BF16 = jnp.bfloat16

GLA_HEADS = 8
GLA_DK = 32
GLA_DV = 64
GLA_RANK = 16
GLA_TAU = 16.0
GLA_CHUNK = 64
NSA_HEADS = 8
NSA_G = 2
NSA_REP = 4
NSA_DH = 64
L_CMP = 32
L_SEL = 64
N_SEL = 16
WINDOW = 512
Q_BLOCK = 128
CONV_W = 3
EPS = 1e-6
NEG = -1e30
SLOPES = tuple(float(2.0 ** -(h + 1)) for h in range(NSA_HEADS))

GQ_W = GLA_HEADS * GLA_DK
GV_W = GLA_HEADS * GLA_DV
NQ_W = NSA_HEADS * NSA_DH
KV_W = 2 * NSA_G * NSA_DH
GATE_OFF = GLA_RANK
SEL_TILE = 256
VMEM_LIMIT = 56 * 1024 * 1024


def _nt(a, b):
    return lax.dot_general(a, b, (((1,), (1,)), ((), ())), preferred_element_type=F32)


def _tn(a, b):
    return lax.dot_general(a, b, (((0,), (0,)), ((), ())), preferred_element_type=F32)


def _dot(a, b):
    return jnp.dot(a, b, preferred_element_type=F32)


def _split3(x):
    hi = x.astype(BF16)
    r1 = x - hi.astype(F32)
    mid = r1.astype(BF16)
    lo = (r1 - mid.astype(F32)).astype(BF16)
    return hi, mid, lo


def _dot_exact_lhs(m_bf16, x):
    hi, mid, lo = _split3(x)
    return _dot(m_bf16, hi) + _dot(m_bf16, mid) + _dot(m_bf16, lo)


def _sigmoid(x):
    return 1.0 / (1.0 + jnp.exp(-x))


def _log_sigmoid(x):
    return jnp.minimum(x, 0.0) - jnp.log(1.0 + jnp.exp(-jnp.abs(x)))


def _rms(x, g):
    ms = jnp.mean(x * x, axis=-1, keepdims=True)
    return x * lax.rsqrt(ms + EPS) * g


def _const_spec(shape):
    nd = len(shape)
    return pl.BlockSpec(shape, lambda *_: (0,) * nd, pipeline_mode=pl.Buffered(1))


def _lane_halves(pair):
    return pair, pltpu.roll(pair, NSA_DH, 1)


def _proj_kernel(x_ref, g_ref, wm_ref, ws_ref, wa_ref, ba_ref, bg_ref,
                 gq_ref, gk_ref, gv_ref, gr_ref, glog_ref, kvc_ref, kvs_ref, kvw_ref, *attn_refs, prompt):
    tm = x_ref.shape[0]
    h = _rms(x_ref[...], g_ref[...]).astype(BF16)

    def mm(a, b):
        return _dot(h, wm_ref[:, a:b])

    o = 0
    gq_ref[...] = mm(o, o + GQ_W) * (GLA_DK ** -0.5); o += GQ_W
    gk_ref[...] = mm(o, o + GQ_W); o += GQ_W
    gv_ref[...] = mm(o, o + GV_W); o += GV_W
    gr_ref[...] = mm(o, o + GV_W); o += GV_W
    nq = mm(o, o + NQ_W) * (NSA_DH ** -0.5); o += NQ_W
    kvc = mm(o, o + KV_W); o += KV_W
    kvs = mm(o, o + KV_W); o += KV_W
    kvw = mm(o, o + KV_W); o += KV_W
    kvs_t = kvs.T
    kvw_t = kvw.T
    kvc_ref[0] = kvc.T
    kvs_ref[0] = kvs_t
    kvw_ref[0] = kvw_t
    small = _dot(h, ws_ref[...])
    z = _dot(small.astype(BF16), wa_ref[...]) + ba_ref[...]
    glog_ref[...] = _log_sigmoid(z) * (1.0 / GLA_TAU)
    gates = _sigmoid(small + bg_ref[...])
    if not prompt:
        nqh_ref, gates_ref, kvs_rows_ref, kvw_rows_ref = attn_refs
        for hh in range(NSA_HEADS):
            nqh_ref[0, hh] = nq[:, hh * NSA_DH:(hh + 1) * NSA_DH].astype(BF16)
        gates_ref[...] = gates
        kvs_rows_ref[...] = kvs
        kvw_rows_ref[...] = kvw
        return
    qa_ref, ksa_ref, kwa_ref, vst_ref, vwt_ref, cm_ref, gt_ref = attn_refs
    lane = lax.broadcasted_iota(jnp.int32, (tm, 128), 1)
    low = lane < NSA_DH
    for j in range(NSA_HEADS // 2):
        for hh, qh in zip((2 * j, 2 * j + 1), _lane_halves(nq[:, j * 128:(j + 1) * 128])):
            qa_ref[0, hh] = jnp.where(low, qh, jnp.where(lane == NSA_DH, SLOPES[hh], 0.0)).astype(BF16)
    pos = pl.program_id(1) * tm + lax.broadcasted_iota(jnp.int32, (tm, 128), 0)
    tile_pos = (pos % SEL_TILE).astype(F32)
    for src, src_t, ka_ref, vt_ref, tile in ((kvs, kvs_t, ksa_ref, vst_ref, SEL_TILE),
                                             (kvw, kvw_t, kwa_ref, vwt_ref, Q_BLOCK)):
        for g, kg in enumerate(_lane_halves(src[:, 0:NSA_G * NSA_DH])):
            ka_ref[0, g] = jnp.where(low, kg, tile_pos).astype(BF16)
        vt = src_t[NSA_G * NSA_DH:, :].astype(BF16)
        for g in range(NSA_G):
            for c in range(tm // tile):
                vt_ref[0, g, c] = vt[g * NSA_DH:(g + 1) * NSA_DH, c * tile:(c + 1) * tile]
    cm_ref[0] = kvc.reshape(tm // L_CMP, L_CMP, KV_W).sum(axis=1) * (1.0 / L_CMP)
    gt_ref[...] = gates.T


def _proj(x3, norm_g, wm, ws, wa, ba, bg, tm, prompt):
    b, t, d = x3.shape
    n = b * t
    nt = t // tm
    x2 = x3.reshape(n, d)
    row = lambda w: pl.BlockSpec((tm, w), lambda bi, ti: (bi * nt + ti, 0))
    sds = jax.ShapeDtypeStruct
    kvt = pl.BlockSpec((1, KV_W, tm), lambda bi, ti: (bi, 0, ti))
    out_shape = [sds((n, GQ_W), F32), sds((n, GQ_W), F32), sds((n, GV_W), F32), sds((n, GV_W), F32),
                 sds((n, GQ_W), F32), sds((b, KV_W, t), F32), sds((b, KV_W, t), F32), sds((b, KV_W, t), F32)]
    out_specs = [row(GQ_W), row(GQ_W), row(GV_W), row(GV_W), row(GQ_W), kvt, kvt, kvt]
    if prompt:
        hm = lambda nh: pl.BlockSpec((1, nh, tm, 128), lambda bi, ti: (bi, 0, ti, 0))
        vt = lambda tile: pl.BlockSpec((1, NSA_G, tm // tile, NSA_DH, tile), lambda bi, ti: (bi, 0, ti, 0, 0))
        out_shape += [sds((b, NSA_HEADS, t, 128), BF16), sds((b, NSA_G, t, 128), BF16), sds((b, NSA_G, t, 128), BF16),
                      sds((b, NSA_G, t // SEL_TILE, NSA_DH, SEL_TILE), BF16),
                      sds((b, NSA_G, t // Q_BLOCK, NSA_DH, Q_BLOCK), BF16),
                      sds((b, t // L_CMP, KV_W), F32), sds((128, n), F32)]
        out_specs += [hm(NSA_HEADS), hm(NSA_G), hm(NSA_G), vt(SEL_TILE), vt(Q_BLOCK),
                      pl.BlockSpec((1, tm // L_CMP, KV_W), lambda bi, ti: (bi, ti, 0)),
                      pl.BlockSpec((128, tm), lambda bi, ti: (0, bi * nt + ti))]
    else:
        out_shape += [sds((b, NSA_HEADS, t, NSA_DH), BF16), sds((n, 128), F32), sds((n, KV_W), F32), sds((n, KV_W), F32)]
        out_specs += [pl.BlockSpec((1, NSA_HEADS, tm, NSA_DH), lambda bi, ti: (bi, 0, ti, 0)), row(128),
                      row(KV_W), row(KV_W)]
    in_specs = [row(d), _const_spec((1, d)), _const_spec(wm.shape), _const_spec(ws.shape),
                _const_spec(wa.shape), _const_spec(ba.shape), _const_spec(bg.shape)]
    return pl.pallas_call(
        functools.partial(_proj_kernel, prompt=prompt), grid=(b, nt), in_specs=in_specs, out_specs=out_specs,
        out_shape=out_shape,
        compiler_params=pltpu.CompilerParams(dimension_semantics=("arbitrary", "arbitrary"),
                                             vmem_limit_bytes=VMEM_LIMIT),
        name="proj")(x2, norm_g, wm, ws, wa, ba, bg)


GLA_CHUNKS_PER_TRIP = 4


def _gla_consts():
    c = GLA_CHUNK
    t = np.arange(c)
    s = t[None, :]
    masks = []
    m = c // 2
    while m >= 1:
        blk = t // (2 * m)
        upper = t > blk * 2 * m + m - 1
        masks.append((upper[:, None] & (~upper)[None, :] & (blk[:, None] == blk[None, :])).astype(np.float32))
        m //= 2
    mcat = np.tril(np.ones((c, c), np.float32))
    lvl = np.stack([np.tile(mk, (1, GLA_HEADS)) for mk in masks])
    eye = np.tile(np.eye(c, dtype=np.float32), (1, GLA_HEADS))
    hk = np.arange(GLA_HEADS * GLA_DK) // GLA_DK
    hc = np.arange(GLA_HEADS * c) // c
    hv = np.arange(GLA_HEADS * GLA_DV) // GLA_DV
    bdk = (hc[:, None] == hk[None, :]).astype(np.float32)
    bdv = (hc[:, None] == hv[None, :]).astype(np.float32)
    bds = (hv[:, None] == hk[None, :]).astype(np.float32)
    hsum = (hk[:, None] == hc[None, :]).astype(np.float32)
    hmean = (hv[:, None] == hv[None, :]).astype(np.float32) / GLA_DV
    return (jnp.asarray(mcat, BF16), jnp.asarray(lvl, F32), jnp.asarray(eye, F32), jnp.asarray(bdk, BF16),
            jnp.asarray(bdv, BF16), jnp.asarray(bds, F32), jnp.asarray(hsum, BF16), jnp.asarray(hmean, BF16))


def _gla_kernel(q_ref, k_ref, v_ref, g_ref, r_ref, gain_ref, mcat_ref, lvl_ref, eye_ref, bdk_ref, bdv_ref,
                bds_ref, hsum_ref, hmean_ref, y_ref, st_ref, s_sc):
    c = GLA_CHUNK
    nlev = lvl_ref.shape[0]
    n_chunks = q_ref.shape[0] // c

    @pl.when(pl.program_id(1) == 0)
    def _():
        s_sc[...] = jnp.zeros_like(s_sc)

    t_row = lax.broadcasted_iota(jnp.int32, (c, 1), 0)
    halves = [c >> (lv + 1) for lv in range(nlev)]
    offs = [t_row - ((t_row // (2 * m)) * (2 * m) + m - 1) for m in halves]

    def ref_rows(b, lv):
        m = halves[lv]
        if 2 * m >= 16:
            return jnp.concatenate([jnp.broadcast_to(b[s0 + m - 1:s0 + m, :], (2 * m, b.shape[1]))
                                    for s0 in range(0, c, 2 * m)], axis=0)
        out = b
        for d in range(-(m - 1), m + 1):
            if d != 0:
                out = jnp.where(offs[lv] == d, pltpu.roll(b, d % c, 0), out)
        return out

    def chunk(ci):
        r0 = pl.multiple_of(ci * c, c)
        q = q_ref[pl.ds(r0, c), :]
        k = k_ref[pl.ds(r0, c), :]
        v = v_ref[pl.ds(r0, c), :].astype(BF16)
        g = g_ref[pl.ds(r0, c), :]
        b = _dot_exact_lhs(mcat_ref[...], g)
        a = _dot((q * k).astype(BF16), hsum_ref[...]) * eye_ref[...]
        for lv in range(nlev):
            delta = b - ref_rows(b, lv)
            upper = offs[lv] > 0
            qt = (q * jnp.exp(jnp.where(upper, delta, 0.0))).astype(BF16)
            kt = (k * jnp.exp(jnp.where(upper, 0.0, -delta))).astype(BF16)
            kbd = jnp.concatenate([kt] * GLA_HEADS, axis=0) * bdk_ref[...]
            a = a + _nt(qt, kbd) * lvl_ref[lv]
        vbd = jnp.concatenate([v] * GLA_HEADS, axis=0) * bdv_ref[...]
        s_t = s_sc[...]
        o = _dot(a.astype(BF16), vbd) + _nt((q * jnp.exp(b)).astype(BF16), s_t.astype(BF16))
        b_last = b[c - 1:c, :]
        kd = (k * jnp.exp(b_last - b)).astype(BF16)
        s_sc[...] = s_t * jnp.exp(b_last) + _tn(v, kd) * bds_ref[...]
        o2 = o * o
        o2h = o2.astype(BF16)
        o2l = (o2 - o2h.astype(F32)).astype(BF16)
        ms2 = _dot(jnp.concatenate([o2h, o2l], axis=0), hmean_ref[...])
        ms = ms2[0:c] + ms2[c:2 * c]
        r = r_ref[pl.ds(r0, c), :]
        y = o * lax.rsqrt(ms + EPS) * gain_ref[...] * (r * _sigmoid(r))
        y_ref[pl.ds(r0, c), :] = y.astype(BF16)

    def chunk_group(pi, carry):
        for j in range(GLA_CHUNKS_PER_TRIP):
            chunk(GLA_CHUNKS_PER_TRIP * pi + j)
        return carry

    lax.fori_loop(0, n_chunks // GLA_CHUNKS_PER_TRIP, chunk_group, 0)

    @pl.when(pl.program_id(1) == pl.num_programs(1) - 1)
    def _():
        st_ref[0] = s_sc[...]


def _gla_prompt(gq, gk, gv, glog, gr, gain, b, t, tc):
    n = b * t
    nt = t // tc
    consts = _gla_consts()
    gain_t = jnp.tile(gain.astype(F32).reshape(1, GLA_DV), (1, GLA_HEADS))
    row = lambda w: pl.BlockSpec((tc, w), lambda bi, ti: (bi * nt + ti, 0))
    in_specs = [row(GQ_W), row(GQ_W), row(GV_W), row(GQ_W), row(GV_W), _const_spec((1, GV_W))]
    in_specs += [_const_spec(a.shape) for a in consts]
    y, st = pl.pallas_call(
        _gla_kernel, grid=(b, nt), in_specs=in_specs,
        out_specs=[row(GV_W), pl.BlockSpec((1, GV_W, GQ_W), lambda bi, ti: (bi, 0, 0))],
        out_shape=[jax.ShapeDtypeStruct((n, GV_W), BF16), jax.ShapeDtypeStruct((b, GV_W, GQ_W), F32)],
        scratch_shapes=[pltpu.VMEM((GV_W, GQ_W), F32)],
        compiler_params=pltpu.CompilerParams(dimension_semantics=("arbitrary", "arbitrary"),
                                             vmem_limit_bytes=VMEM_LIMIT),
        name="gla_prompt")(gq, gk, gv, glog, gr, gain_t, *consts)
    s = jnp.stack([st[:, h * GLA_DV:(h + 1) * GLA_DV, h * GLA_DK:(h + 1) * GLA_DK] for h in range(GLA_HEADS)], axis=1)
    return y, s.transpose(0, 1, 3, 2)


def _gla_step_kernel(q_ref, k_ref, g_ref, v_ref, r_ref, gain_ref, s0_ref, y_ref, s1_ref):
    bb = q_ref.shape[0]
    hd = GLA_HEADS * GLA_DK
    ri = lax.broadcasted_iota(jnp.int32, (hd, hd), 0)
    ci = lax.broadcasted_iota(jnp.int32, (hd, hd), 1)
    eye = ri == ci
    rowhead = lax.broadcasted_iota(jnp.int32, (hd, GLA_DV), 0) // GLA_DK

    def col(x_row):
        return jnp.sum(jnp.where(eye, jnp.broadcast_to(x_row, (hd, hd)), 0.0), axis=1, keepdims=True)

    for i in range(bb):
        q = q_ref[i:i + 1, :]
        k = k_ref[i:i + 1, :]
        eg = jnp.exp(g_ref[i:i + 1, :])
        v3 = v_ref[i]
        s0 = s0_ref[i]
        vrows = jnp.zeros((hd, GLA_DV), F32)
        for h in range(GLA_HEADS):
            vrows = jnp.where(rowhead == h, jnp.broadcast_to(v3[h:h + 1, :], (hd, GLA_DV)), vrows)
        s1_ref[i] = col(eg) * s0 + col(k) * vrows
        t1 = (col(q * eg) * s0).reshape(GLA_HEADS, GLA_DK, GLA_DV).sum(axis=1)
        qk = col(q * k).reshape(GLA_HEADS, GLA_DK, 1).sum(axis=1)
        o = t1 + qk * v3
        ms = jnp.mean(o * o, axis=-1, keepdims=True)
        r3 = r_ref[i]
        y_ref[i] = (o * lax.rsqrt(ms + EPS) * gain_ref[...] * (r3 * _sigmoid(r3))).astype(BF16)


def _gla_step(gq, gk, glog, gv, gr, gain, s0, bb=8):
    n = gq.shape[0]
    hd = GLA_HEADS * GLA_DK
    v3 = gv.reshape(n, GLA_HEADS, GLA_DV)
    r3 = gr.reshape(n, GLA_HEADS, GLA_DV)
    s0r = s0.reshape(n, hd, GLA_DV)
    row = pl.BlockSpec((bb, hd), lambda i: (i, 0))
    h3 = pl.BlockSpec((bb, GLA_HEADS, GLA_DV), lambda i: (i, 0, 0))
    st = pl.BlockSpec((bb, hd, GLA_DV), lambda i: (i, 0, 0))
    y, s1 = pl.pallas_call(
        _gla_step_kernel, grid=(n // bb,),
        in_specs=[row, row, row, h3, h3, _const_spec((1, GLA_DV)), st],
        out_specs=[h3, st],
        out_shape=[jax.ShapeDtypeStruct((n, GLA_HEADS, GLA_DV), BF16), jax.ShapeDtypeStruct((n, hd, GLA_DV), F32)],
        compiler_params=pltpu.CompilerParams(dimension_semantics=("arbitrary",)),
        name="gla_step")(gq, gk, glog, v3, r3, gain.astype(F32).reshape(1, GLA_DV), s0r)
    return y.reshape(n, GV_W), s1


def _topk_mask_t(v, k_sel):
    n = v.shape[0]
    sb = lax.broadcasted_iota(jnp.int32, v.shape, 0)
    sel = jnp.zeros(v.shape, F32)
    for _ in range(k_sel):
        m = jnp.max(v, axis=0, keepdims=True)
        idx = jnp.min(jnp.where(v == m, sb, n), axis=0, keepdims=True)
        hit = sb == idx
        sel = jnp.where(hit, 1.0, sel)
        v = jnp.where(hit, -jnp.inf, v)
    return sel


def _topk_mask(v, k_sel):
    n = v.shape[-1]
    sb = lax.broadcasted_iota(jnp.int32, v.shape, 1)
    sel = jnp.zeros(v.shape, F32)
    idxs = []
    for _ in range(k_sel):
        m = jnp.max(v, axis=-1, keepdims=True)
        idx = jnp.min(jnp.where(v == m, sb, n), axis=-1, keepdims=True)
        hit = sb == idx
        sel = jnp.where(hit, 1.0, sel)
        v = jnp.where(hit, -jnp.inf, v)
        idxs.append(idx)
    return sel, idxs


def _group_queries(q_ref, g):
    return jnp.concatenate([q_ref[0, g * NSA_REP + r] for r in range(NSA_REP)], axis=0)


def _nsa_cmp_kernel(q_ref, cm_ref, oc_ref, selm_ref, any_ref):
    nc = cm_ref.shape[1]
    half = nc // 2
    ns = half
    q0 = pl.program_id(1) * Q_BLOCK
    qpos = q0 + lax.broadcasted_iota(jnp.int32, (1, Q_BLOCK), 1)
    row = lax.broadcasted_iota(jnp.int32, (nc, 1), 0)
    blk = jnp.where(row < half, 2 * row, 2 * (row - half) + 1)
    mask_c = ((blk + 1) * L_CMP - 1) <= qpos
    dist_c = qpos.astype(F32) - (blk.astype(F32) * L_CMP + (L_CMP - 1) / 2.0)
    sb = lax.broadcasted_iota(jnp.int32, (ns, 1), 0)
    cur = qpos // L_SEL
    forced = (sb == 0) | (sb == cur) | (sb == cur - 1)
    future = sb * L_SEL > qpos
    cm = cm_ref[0]
    low = lax.broadcasted_iota(jnp.int32, (nc, 128), 1) < NSA_DH
    vt_all = cm[:, NSA_G * NSA_DH:].T
    ones = jnp.ones((8, Q_BLOCK), BF16)
    for g, kg in enumerate(_lane_halves(cm[:, 0:NSA_G * NSA_DH])):
        kc = jnp.where(low, kg, 0.0).astype(BF16)
        vct = vt_all[g * NSA_DH:(g + 1) * NSA_DH, :].astype(BF16)
        s_all = _nt(kc, _group_queries(q_ref, g))
        imp = jnp.zeros((nc, Q_BLOCK), F32)
        ps = []
        for r in range(NSA_REP):
            h = g * NSA_REP + r
            s = s_all[:, r * Q_BLOCK:(r + 1) * Q_BLOCK] - SLOPES[h] * dist_c
            s = jnp.where(mask_c, s, NEG)
            e = jnp.exp(s - jnp.max(s, axis=0, keepdims=True))
            p = jnp.where(mask_c, e * (1.0 / jnp.sum(e, axis=0, keepdims=True)), 0.0)
            ps.append(p.astype(BF16))
            imp = imp + p
        oct = _dot(vct, jnp.concatenate(ps, axis=1))
        for r in range(NSA_REP):
            oc_ref[0, g * NSA_REP + r] = oct[:, r * Q_BLOCK:(r + 1) * Q_BLOCK]
        imp2 = imp[:half] + imp[half:]
        v = jnp.where(future, -jnp.inf, jnp.where(forced, jnp.inf, imp2))
        sel = _topk_mask_t(v, min(N_SEL, ns))
        selm_ref[0, g] = sel
        any_ref[0, g:g + 1, :] = _nt(ones, sel.astype(BF16))[0:1]


def _nsa_cmp(qa, cm_perm):
    b, _, t, _ = qa.shape
    nqb = t // Q_BLOCK
    nc = t // L_CMP
    ns = t // L_SEL
    return pl.pallas_call(
        _nsa_cmp_kernel, grid=(b, nqb),
        in_specs=[pl.BlockSpec((1, NSA_HEADS, Q_BLOCK, 128), lambda bi, i: (bi, 0, i, 0)),
                  pl.BlockSpec((1, nc, KV_W), lambda bi, i: (bi, 0, 0))],
        out_specs=[pl.BlockSpec((1, NSA_HEADS, NSA_DH, Q_BLOCK), lambda bi, i: (bi, 0, 0, i)),
                   pl.BlockSpec((1, NSA_G, ns, Q_BLOCK), lambda bi, i: (bi, 0, 0, i)),
                   pl.BlockSpec((1, NSA_G, ns), lambda bi, i: (bi * nqb + i, 0, 0))],
        out_shape=[jax.ShapeDtypeStruct((b, NSA_HEADS, NSA_DH, t), F32),
                   jax.ShapeDtypeStruct((b, NSA_G, ns, t), F32),
                   jax.ShapeDtypeStruct((b * nqb, NSA_G, ns), F32)],
        compiler_params=pltpu.CompilerParams(dimension_semantics=("arbitrary", "arbitrary"),
                                             vmem_limit_bytes=VMEM_LIMIT),
        name="nsa_cmp")(qa, cm_perm)


WIN_TILES = (WINDOW + Q_BLOCK) // Q_BLOCK


def _nsa_sel_kernel(bits_ref, q_ref, ksa_ref, kwa_ref, vst_ref, vwt_ref, selm_ref, oc_ref, gt_ref, y_ref,
                    m_sc, l_sc, acc_sc):
    t_len = ksa_ref.shape[2]
    nqb = pl.num_programs(1)
    bi = pl.program_id(0)
    i = pl.program_id(1)
    q0 = i * Q_BLOCK
    qpos = q0 + lax.broadcasted_iota(jnp.int32, (1, Q_BLOCK), 1)
    diag = q0 // SEL_TILE
    nwords = (t_len // SEL_TILE + 15) // 16
    per = SEL_TILE // L_SEL
    wt0 = jnp.clip(i - WINDOW // Q_BLOCK, 0, t_len // Q_BLOCK - WIN_TILES)
    wstart = pl.multiple_of(wt0 * Q_BLOCK, Q_BLOCK)
    wkeys = WIN_TILES * Q_BLOCK
    wdist = qpos - (wstart + lax.broadcasted_iota(jnp.int32, (wkeys, 1), 0))
    wvalid = (wdist >= 0) & (wdist <= WINDOW)
    kiota = lax.broadcasted_iota(jnp.int32, (SEL_TILE, 1), 0)
    gts = gt_ref[...]
    q_alls = [_group_queries(q_ref, g) for g in range(NSA_G)]
    m_sc[...] = jnp.full(m_sc.shape, NEG, F32)
    l_sc[...] = jnp.zeros_like(l_sc)
    acc_sc[...] = jnp.zeros_like(acc_sc)
    base = (bi * nqb + i) * NSA_G * nwords

    def attend_tile(g, t, causal):
        ks = pl.multiple_of(t * SEL_TILE, SEL_TILE)
        s_all = _nt(ksa_ref[0, g, pl.ds(ks, SEL_TILE), :], q_alls[g])
        mk = jnp.concatenate([jnp.broadcast_to(selm_ref[0, g, pl.ds(t * per + j, 1), :], (L_SEL, Q_BLOCK))
                              for j in range(per)], axis=0) > 0.5
        if causal:
            mk = mk & ((ks + kiota) <= qpos)
        shift = (ks - q0).astype(F32)
        ps, alphas = [], []
        for r in range(NSA_REP):
            cs = slice(r * Q_BLOCK, (r + 1) * Q_BLOCK)
            s = jnp.where(mk, s_all[:, cs] + SLOPES[g * NSA_REP + r] * shift, NEG)
            m_old = m_sc[g, :, cs]
            m_new = jnp.maximum(m_old, jnp.max(s, axis=0, keepdims=True))
            alpha = jnp.exp(m_old - m_new)
            p = jnp.exp(s - m_new)
            l_sc[g, :, cs] = alpha * l_sc[g, :, cs] + jnp.sum(p, axis=0, keepdims=True)
            m_sc[g, :, cs] = m_new
            ps.append(p.astype(BF16))
            alphas.append(alpha)
        acc_sc[g] = jnp.concatenate(alphas, axis=1) * acc_sc[g] + _dot(vst_ref[0, g, t], jnp.concatenate(ps, axis=1))

    def tile(t, carry):
        flag = bits_ref[base + t // 16]
        for g in range(1, NSA_G):
            flag = flag | bits_ref[base + g * nwords + t // 16]

        @pl.when((lax.shift_right_logical(flag, t % 16) & 1) == 1)
        def _():
            for g in range(NSA_G):
                attend_tile(g, t, False)

        return carry

    lax.fori_loop(0, diag, tile, 0)
    for g in range(NSA_G):
        attend_tile(g, diag, True)

    for g in range(NSA_G):
        q_all = q_alls[g]
        o_s = acc_sc[g] * (1.0 / l_sc[g])
        s_all = _nt(kwa_ref[0, g, pl.ds(wstart, wkeys), :], q_all)
        vw = jnp.concatenate([vwt_ref[0, g, wt0 + j] for j in range(WIN_TILES)], axis=1)
        shifts = [(((wstart + j * Q_BLOCK) // SEL_TILE) * SEL_TILE - q0).astype(F32) for j in range(WIN_TILES)]
        ps = []
        for r in range(NSA_REP):
            slope = SLOPES[g * NSA_REP + r]
            s = jnp.concatenate([s_all[j * Q_BLOCK:(j + 1) * Q_BLOCK, r * Q_BLOCK:(r + 1) * Q_BLOCK] + slope * shifts[j]
                                 for j in range(WIN_TILES)], axis=0)
            s = jnp.where(wvalid, s, NEG)
            e = jnp.exp(s - jnp.max(s, axis=0, keepdims=True))
            ps.append((e * (1.0 / jnp.sum(e, axis=0, keepdims=True))).astype(BF16))
        o_w = _dot(vw, jnp.concatenate(ps, axis=1))

        outs = []
        for r in range(NSA_REP):
            h = g * NSA_REP + r
            c0 = GATE_OFF + h * 3
            cs = slice(r * Q_BLOCK, (r + 1) * Q_BLOCK)
            outs.append(gts[c0:c0 + 1, :] * oc_ref[0, h] + gts[c0 + 1:c0 + 2, :] * o_s[:, cs]
                        + gts[c0 + 2:c0 + 3, :] * o_w[:, cs])
        gw = NSA_REP * NSA_DH
        y_ref[:, g * gw:(g + 1) * gw] = jnp.concatenate(outs, axis=0).T.astype(BF16)


def _nsa_sel(bits, qa, ksa, kwa, vst, vwt, selm, oc, gates_t):
    b, _, t, _ = qa.shape
    nqb = t // Q_BLOCK
    ns = t // L_SEL
    whole = lambda a: pl.BlockSpec((1,) + a.shape[1:], lambda bi, i, *_: (bi,) + (0,) * (a.ndim - 1),
                                   pipeline_mode=pl.Buffered(1))
    cols = lambda lead: pl.BlockSpec((1,) + lead + (Q_BLOCK,), lambda bi, i, *_: (bi,) + (0,) * len(lead) + (i,))
    grid_spec = pltpu.PrefetchScalarGridSpec(
        num_scalar_prefetch=1, grid=(b, nqb),
        in_specs=[pl.BlockSpec((1, NSA_HEADS, Q_BLOCK, 128), lambda bi, i, *_: (bi, 0, i, 0)),
                  whole(ksa), whole(kwa), whole(vst), whole(vwt),
                  cols((NSA_G, ns)), cols((NSA_HEADS, NSA_DH)),
                  pl.BlockSpec((128, Q_BLOCK), lambda bi, i, *_: (0, bi * nqb + i))],
        out_specs=[pl.BlockSpec((Q_BLOCK, NQ_W), lambda bi, i, *_: (bi * nqb + i, 0))],
        scratch_shapes=[pltpu.VMEM((NSA_G, 1, NSA_REP * Q_BLOCK), F32), pltpu.VMEM((NSA_G, 1, NSA_REP * Q_BLOCK), F32),
                        pltpu.VMEM((NSA_G, NSA_DH, NSA_REP * Q_BLOCK), F32)])
    return pl.pallas_call(
        _nsa_sel_kernel, grid_spec=grid_spec,
        out_shape=[jax.ShapeDtypeStruct((b * t, NQ_W), BF16)],
        compiler_params=pltpu.CompilerParams(dimension_semantics=("arbitrary", "arbitrary"),
                                             vmem_limit_bytes=VMEM_LIMIT),
        name="nsa_sel")(bits, qa, ksa, kwa, vst, vwt, selm, oc, gates_t)[0]


def _tile_bits(anyf, ns):
    r = anyf.shape[0]
    per = SEL_TILE // L_SEL
    ntile = ns // per
    nwords = (ntile + 15) // 16
    tf = (anyf.reshape(r, NSA_G, ntile, per).max(axis=-1) > 0.5).astype(jnp.int32)
    tf = jnp.pad(tf, ((0, 0), (0, 0), (0, nwords * 16 - ntile))).reshape(r, NSA_G, nwords, 16)
    return jnp.sum(tf << jnp.arange(16, dtype=jnp.int32), axis=-1).reshape(-1).astype(jnp.int32)


PAGES_PER_STEP = 8


def _page_mean_kernel(pt_ref, cache_ref, avg_ref, o_ref, buf, sems, *, npg, steps_per_tile):
    bi = pl.program_id(0)
    n_chunks = npg // PAGES_PER_STEP
    psz = cache_ref.shape[-1]
    cols = avg_ref.shape[-1]

    def page_copy(seq, c, p):
        page = pt_ref[seq * npg + c * PAGES_PER_STEP + p]
        return pltpu.make_async_copy(cache_ref.at[page], buf.at[c % 2, p], sems.at[c % 2, p])

    def start_chunk(seq, c):
        for p in range(PAGES_PER_STEP):
            page_copy(seq, c, p).start()

    @pl.when(bi == 0)
    def _():
        start_chunk(bi, 0)

    acc = None
    for c in range(n_chunks):
        if c + 1 < n_chunks:
            start_chunk(bi, c + 1)
        else:
            @pl.when(bi + 1 < pl.num_programs(0))
            def _():
                start_chunk(bi + 1, 0)
        for p in range(PAGES_PER_STEP):
            page_copy(bi, c, p).wait()
        x = jnp.concatenate([buf[c % 2, p].reshape(KV_W, psz) for p in range(PAGES_PER_STEP)], axis=1)
        part = _dot(x.astype(BF16), avg_ref[c % steps_per_tile])
        acc = part if c % steps_per_tile == 0 else acc + part
        if c % steps_per_tile == steps_per_tile - 1:
            tile = c // steps_per_tile
            o_ref[0, :, tile * cols:(tile + 1) * cols] = acc


def _page_means(cache_t, page_table):
    db, npg = page_table.shape
    psz = cache_t.shape[-1]
    per = psz // L_CMP
    nblk = npg * per
    cols = min(128, nblk)
    step_blocks = PAGES_PER_STEP * per
    steps_per_tile = cols // step_blocks
    n_chunks = npg // PAGES_PER_STEP
    assert npg % PAGES_PER_STEP == 0 and n_chunks % 2 == 0 and n_chunks % steps_per_tile == 0
    tok = np.arange(PAGES_PER_STEP * psz)
    avg = np.zeros((steps_per_tile, PAGES_PER_STEP * psz, cols), np.float32)
    for s in range(steps_per_tile):
        avg[s, tok, s * step_blocks + tok // L_CMP] = 1.0 / L_CMP

    grid_spec = pltpu.PrefetchScalarGridSpec(
        num_scalar_prefetch=1, grid=(db,),
        in_specs=[pl.BlockSpec(memory_space=pl.ANY),
                  pl.BlockSpec(avg.shape, lambda bi, pt: (0, 0, 0), pipeline_mode=pl.Buffered(1))],
        out_specs=[pl.BlockSpec((1, KV_W, nblk), lambda bi, pt: (bi, 0, 0))],
        scratch_shapes=[pltpu.VMEM((2, PAGES_PER_STEP) + cache_t.shape[1:], F32),
                        pltpu.SemaphoreType.DMA((2, PAGES_PER_STEP))])
    return pl.pallas_call(
        functools.partial(_page_mean_kernel, npg=npg, steps_per_tile=steps_per_tile), grid_spec=grid_spec,
        out_shape=[jax.ShapeDtypeStruct((db, KV_W, nblk), F32)],
        compiler_params=pltpu.CompilerParams(dimension_semantics=("arbitrary",)),
        name="page_means")(page_table.reshape(-1), cache_t, jnp.asarray(avg, BF16))[0]


def _slope_col():
    row = lax.broadcasted_iota(jnp.int32, (NSA_HEADS, 1), 0)
    sl = jnp.zeros((NSA_HEADS, 1), F32)
    for h in range(NSA_HEADS):
        sl = jnp.where(row == h, SLOPES[h], sl)
    return row, sl


def _nsa_cmp_step_kernel(q_ref, cm_ref, oc_ref, idx_ref, *, past, k_sel):
    bb = q_ref.shape[0]
    nc = cm_ref.shape[2]
    half = nc // 2
    blk = lax.broadcasted_iota(jnp.int32, (1, nc), 1)
    pair = jnp.where(lax.broadcasted_iota(jnp.int32, (nc, half), 0) // (L_SEL // L_CMP)
                     == lax.broadcasted_iota(jnp.int32, (nc, half), 1), 1.0, 0.0).astype(BF16)
    mask_c = ((blk + 1) * L_CMP - 1) <= past
    dist_c = float(past) - (blk.astype(F32) * L_CMP + (L_CMP - 1) / 2.0)
    row, sl = _slope_col()
    imps = []
    for i in range(bb):
        q8 = q_ref[i]
        oc = jnp.zeros((NSA_HEADS, NSA_DH), F32)
        for g in range(NSA_G):
            kct = cm_ref[i, g * NSA_DH:(g + 1) * NSA_DH, :].astype(BF16)
            vct = cm_ref[i, (NSA_G + g) * NSA_DH:(NSA_G + g + 1) * NSA_DH, :].astype(BF16)
            s = _dot(q8, kct) - sl * dist_c
            s = jnp.where(mask_c, s, NEG)
            e = jnp.exp(s - jnp.max(s, axis=-1, keepdims=True))
            p = jnp.where(mask_c, e / jnp.sum(e, axis=-1, keepdims=True), 0.0)
            ingroup = (row // NSA_REP) == g
            oc = jnp.where(ingroup, _nt(p.astype(BF16), vct), oc)
            hi, mid, lo = _split3(jnp.where(ingroup, p, 0.0))
            imps.append(jnp.sum(_dot(hi, pair) + _dot(mid, pair) + _dot(lo, pair), axis=0, keepdims=True))
        oc_ref[i] = oc
    imp = jnp.concatenate(imps, axis=0)
    sb = lax.broadcasted_iota(jnp.int32, imp.shape, 1)
    forced = (sb == 0) | (sb == half - 1)
    _, idxs = _topk_mask(jnp.where(forced, jnp.inf, imp), k_sel)
    lane = lax.broadcasted_iota(jnp.int32, (bb * NSA_G, k_sel), 1)
    acc = jnp.zeros((bb * NSA_G, k_sel), jnp.int32)
    for t, ix in enumerate(idxs):
        acc = jnp.where(lane == t, ix, acc)
    for i in range(bb):
        idx_ref[i] = acc[i * NSA_G:(i + 1) * NSA_G, :]


def _nsa_cmp_step(q3, cm_s, past, k_sel, bb=8):
    db = q3.shape[0]
    nc = cm_s.shape[2]
    return pl.pallas_call(
        functools.partial(_nsa_cmp_step_kernel, past=past, k_sel=k_sel), grid=(db // bb,),
        in_specs=[pl.BlockSpec((bb, NSA_HEADS, NSA_DH), lambda bi: (bi, 0, 0)),
                  pl.BlockSpec((bb, KV_W, nc), lambda bi: (bi, 0, 0))],
        out_specs=[pl.BlockSpec((bb, NSA_HEADS, NSA_DH), lambda bi: (bi, 0, 0)),
                   pl.BlockSpec((bb, NSA_G, k_sel), lambda bi: (bi, 0, 0))],
        out_shape=[jax.ShapeDtypeStruct((db, NSA_HEADS, NSA_DH), F32),
                   jax.ShapeDtypeStruct((db, NSA_G, k_sel), jnp.int32)],
        compiler_params=pltpu.CompilerParams(dimension_semantics=("arbitrary",), vmem_limit_bytes=VMEM_LIMIT),
        name="nsa_cmp_step")(q3, cm_s)


def _nsa_sel_step_kernel(pt_ref, ix_ref, q_ref, oc_ref, gates_ref, ks_new_ref, kw_new_ref, kwt_ref, win_ref, cache_ref,
                         y_ref, wout_ref, blocks, sems, *, past, k_sel, npg, per_page):
    bi = pl.program_id(0)
    w_len = win_ref.shape[2]
    psz = cache_ref.shape[-1]
    n_sel = NSA_G * k_sel

    def page_copy(seq, a, j):
        n = ix_ref[seq * n_sel + j]
        page = pt_ref[seq * npg + n // per_page]
        return pltpu.make_async_copy(cache_ref.at[page, a, j // k_sel], blocks.at[seq % 2, a, j],
                                     sems.at[seq % 2, a, j])

    def start_gather(seq):
        for a in range(2):
            for j in range(n_sel):
                page_copy(seq, a, j).start()

    @pl.when(bi == 0)
    def _():
        start_gather(bi)

    @pl.when(bi + 1 < pl.num_programs(0))
    def _():
        start_gather(bi + 1)

    slot = bi % 2
    row, sl = _slope_col()
    q8 = q_ref[0]
    q8f = q8.astype(F32)
    gts = gates_ref[0]
    gcol = []
    for c in range(3):
        col = jnp.zeros((NSA_HEADS, 1), F32)
        for h in range(NSA_HEADS):
            c0 = GATE_OFF + h * 3 + c
            col = jnp.where(row == h, gts[:, c0:c0 + 1], col)
        gcol.append(col)
    win = win_ref[0]
    ks_new = ks_new_ref[0]
    kw_new = kw_new_ref[0]
    wpos = (past - w_len) + lax.broadcasted_iota(jnp.int32, (1, w_len), 1)
    wdist = past - wpos
    wvalid = (wdist <= WINDOW) & (wpos >= 0)

    def joint_attend(s, valid, k_new, v_new, v_t):
        s_new = jnp.sum(q8f * k_new.astype(BF16).astype(F32), axis=-1, keepdims=True)
        s = jnp.where(valid, s, NEG)
        m = jnp.maximum(jnp.max(s, axis=-1, keepdims=True), s_new)
        e = jnp.where(valid, jnp.exp(s - m), 0.0)
        en = jnp.exp(s_new - m)
        l = jnp.sum(e, axis=-1, keepdims=True) + en
        return _nt((e / l).astype(BF16), v_t) + (en / l).astype(BF16).astype(F32) * v_new.astype(BF16).astype(F32)

    ksl = [slice(g * NSA_DH, (g + 1) * NSA_DH) for g in range(NSA_G)]
    vsl = [slice((NSA_G + g) * NSA_DH, (NSA_G + g + 1) * NSA_DH) for g in range(NSA_G)]
    o_w = []
    for g in range(NSA_G):
        s = _dot(q8, win[ksl[g], :].astype(BF16)) - sl * wdist.astype(F32)
        o_w.append(joint_attend(s, wvalid, kw_new[:, ksl[g]], kw_new[:, vsl[g]], win[vsl[g], :].astype(BF16)))
    kwt = kwt_ref[...]
    new_col = jnp.sum(jnp.where(lax.broadcasted_iota(jnp.int32, kwt.shape, 1) == bi, kwt, 0.0), axis=1, keepdims=True)
    wlane = lax.broadcasted_iota(jnp.int32, (1, w_len), 1)
    wout_ref[0] = jnp.where(wlane == w_len - 1, new_col, pltpu.roll(win, w_len - 1, 1))
    for a in range(2):
        for j in range(n_sel):
            page_copy(bi, a, j).wait()
    lane = lax.broadcasted_iota(jnp.int32, (1, k_sel * psz), 1)
    within = lane % psz
    o = jnp.zeros((NSA_HEADS, NSA_DH), F32)
    for g in range(NSA_G):
        k_t = jnp.concatenate([blocks[slot, 0, g * k_sel + t] for t in range(k_sel)], axis=1).astype(BF16)
        v_t = jnp.concatenate([blocks[slot, 1, g * k_sel + t] for t in range(k_sel)], axis=1).astype(BF16)
        page_pos = jnp.zeros_like(lane)
        half = jnp.zeros_like(lane)
        for t in range(k_sel):
            n = ix_ref[bi * n_sel + g * k_sel + t]
            here = lane // psz == t
            page_pos = jnp.where(here, (n // per_page) * psz, page_pos)
            half = jnp.where(here, n % per_page, half)
        dist = past - (page_pos + within)
        valid = (within // L_SEL == half) & (dist >= 0)
        s = _dot(q8, k_t) - sl * dist.astype(F32)
        o_s = joint_attend(s, valid, ks_new[:, ksl[g]], ks_new[:, vsl[g]], v_t)
        og = gcol[0] * oc_ref[0] + gcol[1] * o_s + gcol[2] * o_w[g]
        o = jnp.where((row // NSA_REP) == g, og, o)
    y_ref[0] = o.astype(BF16)


def _nsa_sel_step(page_table, idx, q3, oc, gates3, ks_new3, kw_new3, kw_new_t, win_t, cache_t, past, k_sel):
    db, npg = page_table.shape
    w_len = win_t.shape[2]
    psz = cache_t.shape[-1]
    per_page = psz // L_SEL

    b3 = lambda w: pl.BlockSpec((1, 1, w), lambda bi, *_: (bi, 0, 0))
    h3 = pl.BlockSpec((1, NSA_HEADS, NSA_DH), lambda bi, *_: (bi, 0, 0))
    wsp = pl.BlockSpec((1, KV_W, w_len), lambda bi, *_: (bi, 0, 0))
    grid_spec = pltpu.PrefetchScalarGridSpec(
        num_scalar_prefetch=2, grid=(db,),
        in_specs=[h3, h3, b3(128), b3(KV_W), b3(KV_W),
                  pl.BlockSpec(kw_new_t.shape, lambda bi, *_: (0, 0), pipeline_mode=pl.Buffered(1)),
                  wsp, pl.BlockSpec(memory_space=pl.ANY)],
        out_specs=[h3, wsp],
        scratch_shapes=[pltpu.VMEM((2, 2, NSA_G * k_sel, NSA_DH, psz), F32),
                        pltpu.SemaphoreType.DMA((2, 2, NSA_G * k_sel))])
    return pl.pallas_call(
        functools.partial(_nsa_sel_step_kernel, past=past, k_sel=k_sel, npg=npg, per_page=per_page),
        grid_spec=grid_spec,
        out_shape=[jax.ShapeDtypeStruct((db, NSA_HEADS, NSA_DH), BF16), jax.ShapeDtypeStruct((db, KV_W, w_len), F32)],
        compiler_params=pltpu.CompilerParams(dimension_semantics=("arbitrary",), vmem_limit_bytes=VMEM_LIMIT),
        name="nsa_sel_step")(page_table.reshape(-1), idx.reshape(-1), q3, oc, gates3, ks_new3, kw_new3, kw_new_t,
                             win_t, cache_t)


FFN_CHUNK = 256


def _gelu(x):
    return 0.5 * x * (1.0 + lax.erf(x * (2.0 ** -0.5)))


def _ffn_kernel(*refs, seq_mode, tiles_per_seq):
    if seq_mode:
        (x_ref, yg_ref, yn_ref, wo_ref, nf_ref, wup_ref, cw_ref, cb_ref, wdn_ref, nl_ref,
         y_ref, conv_ref, abuf) = refs
    else:
        (x_ref, yg_ref, yn_ref, wo_ref, nf_ref, wup_ref, cw_ref, cb_ref, wdn_ref, nl_ref, prev_ref,
         y_ref, conv_ref) = refs
    tm = x_ref.shape[0]
    dff = wdn_ref.shape[0]
    gw = yg_ref.shape[1]
    x1 = x_ref[...] + _dot(yg_ref[...], wo_ref[0:gw, :]) + _dot(yn_ref[...], wo_ref[gw:, :])
    h2 = _rms(x1, nf_ref[...]).astype(BF16)
    if seq_mode:
        t_in = pl.program_id(0) % tiles_per_seq

        @pl.when(t_in == 0)
        def _():
            abuf[0:8, :] = jnp.zeros((8, dff), F32)

    acc = jnp.zeros(x1.shape, F32)
    for c0 in range(0, dff, FFN_CHUNK):
        cs = slice(c0, c0 + FFN_CHUNK)
        a = _dot(h2, wup_ref[:, cs])
        bgate = _dot(h2, wup_ref[:, dff + c0:dff + c0 + FFN_CHUNK])
        if seq_mode:
            abuf[8:8 + tm, cs] = a
            am1 = abuf[7:7 + tm, cs]
            am2 = abuf[6:6 + tm, cs]
        else:
            am2 = prev_ref[:, cs]
            am1 = prev_ref[:, dff + c0:dff + c0 + FFN_CHUNK]
            conv_ref[:, cs] = am1
            conv_ref[:, dff + c0:dff + c0 + FFN_CHUNK] = a
        conv = cb_ref[:, cs] + cw_ref[0:1, cs] * am2 + cw_ref[1:2, cs] * am1 + cw_ref[2:3, cs] * a
        act = (_gelu(conv) * bgate).astype(BF16)
        acc = acc + _dot(act, wdn_ref[cs, :])
    if seq_mode:
        @pl.when(t_in == tiles_per_seq - 1)
        def _():
            conv_ref[0] = abuf[8 + tm - (CONV_W - 1):8 + tm, :]

        abuf[0:8, :] = abuf[tm:tm + 8, :]
    y_ref[...] = _rms(x1 + acc, nl_ref[...])


def _ffn(x2, yg, yn, wo, nf, wup, cw, cb, wdn, nl, prev, b, t, tm):
    n, d = x2.shape
    dff = wdn.shape[0]
    seq_mode = prev is None
    row = lambda w: pl.BlockSpec((tm, w), lambda i: (i, 0))
    in_specs = [row(d), row(yg.shape[1]), row(yn.shape[1]), _const_spec(wo.shape), _const_spec((1, d)),
                _const_spec(wup.shape), _const_spec(cw.shape), _const_spec((1, dff)), _const_spec(wdn.shape),
                _const_spec((1, d))]
    args = [x2, yg, yn, wo, nf, wup, cw, cb, wdn, nl]
    if seq_mode:
        tps = t // tm
        conv_shape = jax.ShapeDtypeStruct((b, CONV_W - 1, dff), F32)
        conv_spec = pl.BlockSpec((1, CONV_W - 1, dff), lambda i: (i // tps, 0, 0))
        scratch = [pltpu.VMEM((8 + tm, dff), F32)]
    else:
        tps = 1
        in_specs.append(row(2 * dff))
        args.append(prev)
        conv_shape = jax.ShapeDtypeStruct((n, 2 * dff), F32)
        conv_spec = row(2 * dff)
        scratch = []
    return pl.pallas_call(
        functools.partial(_ffn_kernel, seq_mode=seq_mode, tiles_per_seq=tps), grid=(n // tm,),
        in_specs=in_specs, out_specs=[row(d), conv_spec],
        out_shape=[jax.ShapeDtypeStruct((n, d), F32), conv_shape], scratch_shapes=scratch,
        compiler_params=pltpu.CompilerParams(dimension_semantics=("arbitrary",), vmem_limit_bytes=VMEM_LIMIT),
        name="ffn_seq" if seq_mode else "ffn_step")(*args)


def _pack_in_weights(w_in, w_gla_a, b_gla_a, b_nsa_gate):
    sizes = (GQ_W, GQ_W, GV_W, GLA_RANK, GV_W, NQ_W, KV_W, KV_W, KV_W, 3 * NSA_HEADS)
    offs = np.concatenate([[0], np.cumsum(sizes)])
    part = lambda i: w_in[:, offs[i]:offs[i + 1]]
    wm = jnp.concatenate([part(0), part(1), part(2), part(4), part(5), part(6), part(7), part(8)], axis=1).astype(BF16)
    small = jnp.concatenate([part(3), part(9)], axis=1)
    ws = jnp.pad(small, ((0, 0), (0, 128 - small.shape[1]))).astype(BF16)
    wa = jnp.pad(w_gla_a, ((0, 128 - GLA_RANK), (0, 0))).astype(BF16)
    ba = b_gla_a.reshape(1, GQ_W).astype(F32)
    bg = jnp.pad(b_nsa_gate.reshape(1, -1), ((0, 0), (GATE_OFF, 128 - GATE_OFF - 3 * NSA_HEADS))).astype(F32)
    return wm, ws, wa, ba, bg


def kernel(x_prompt, x_sample, cache_kv_cmp, cache_kv_sel, cache_kv_win, state_gla, state_conv, page_table,
           norm_mix, w_in, w_gla_a, b_gla_a, gla_norm, b_nsa_gate, w_out, norm_ffn, w_ffn_up, conv_w, conv_b,
           w_ffn_down, norm_final):
    bp, seq, d = x_prompt.shape
    db, dseq, _ = x_sample.shape
    assert dseq == 1 and norm_mix.shape[0] == 1
    npg = page_table.shape[1]
    psz = cache_kv_cmp.shape[2]
    past = npg * psz
    n_phys = cache_kv_cmp.shape[1]
    w_buf = cache_kv_win.shape[2]
    dff = w_ffn_down.shape[1]
    assert past % L_SEL == 0 and w_buf == WINDOW and past >= WINDOW and seq >= WINDOW + Q_BLOCK
    tm = min(512, seq)

    wm, ws, wa, ba, bg = _pack_in_weights(w_in[0], w_gla_a[0], b_gla_a[0], b_nsa_gate[0])
    nmix = norm_mix[0].reshape(1, d).astype(F32)
    wo = w_out[0].astype(BF16)
    nf = norm_ffn[0].reshape(1, d).astype(F32)
    wup = w_ffn_up[0].astype(BF16)
    cw = conv_w[0].astype(F32)
    cb = conv_b[0].reshape(1, dff).astype(F32)
    wdn = w_ffn_down[0].astype(BF16)
    nl = norm_final.reshape(1, d).astype(F32)

    (gq, gk, gv, gr, glog, kvc, kvs, kvw, qa, ksa, kwa, vst, vwt, cm, gates_t) = _proj(
        x_prompt, nmix, wm, ws, wa, ba, bg, tm, True)
    y_gla, s_p = _gla_prompt(gq, gk, gv, glog, gr, gla_norm[0], bp, seq, tm)
    cm_perm = jnp.concatenate([cm[:, 0::2], cm[:, 1::2]], axis=1)
    oc, selm, anyf = _nsa_cmp(qa, cm_perm)
    bits = _tile_bits(anyf, seq // L_SEL)
    y_nsa = _nsa_sel(bits, qa, ksa, kwa, vst, vwt, selm, oc, gates_t)
    y_p, conv_p = _ffn(x_prompt.reshape(bp * seq, d), y_gla, y_nsa, wo, nf, wup, cw, cb, wdn, nl, None, bp, seq, tm)
    w_keep = min(WINDOW, seq)

    def kv_rows(a_t):
        n_b, _, n_t = a_t.shape
        return a_t.reshape(1, n_b, 2, NSA_G, NSA_DH, n_t).transpose(0, 1, 5, 2, 3, 4)

    def token_minor(cache):
        return cache.transpose(0, 2, 3, 4, 1)

    out_prompt = (y_p.reshape(bp, seq, d), kv_rows(kvc), kv_rows(kvs), kv_rows(kvw[:, :, seq - w_keep:]),
                  s_p[None], conv_p[None])

    xs3 = x_sample.reshape(1, db, d)
    (gq, gk, gv, gr, glog, kvc_st, kvs_st, kvw_st, nqh, gates, kvs_s, kvw_s) = _proj(
        xs3, nmix, wm, ws, wa, ba, bg, db, False)
    y_gla_s, s_s = _gla_step(gq, gk, glog, gv, gr, gla_norm[0], state_gla[0].astype(F32))
    q3 = nqh[0].transpose(1, 0, 2)
    cm_s = _page_means(token_minor(cache_kv_cmp[0]), page_table)
    k_sel = min(N_SEL, past // L_SEL + 1) - 1
    oc_s, idx = _nsa_cmp_step(q3, cm_s, past, k_sel, bb=min(8, db))
    yn_s, win_out = _nsa_sel_step(page_table, idx, q3, oc_s, gates.reshape(db, 1, 128), kvs_s.reshape(db, 1, KV_W),
                                  kvw_s.reshape(db, 1, KV_W), kvw_st[0],
                                  token_minor(cache_kv_win[0]).reshape(db, KV_W, w_buf),
                                  token_minor(cache_kv_sel[0]), past, k_sel)
    y_s, conv_s = _ffn(x_sample.reshape(db, d), y_gla_s, yn_s.reshape(db, NQ_W), wo, nf, wup, cw, cb, wdn, nl,
                       state_conv[0].reshape(db, (CONV_W - 1) * dff).astype(F32), db, 1, db)
    sample_rows = lambda a_t: a_t[0].T.reshape(1, db, 1, 2, NSA_G, NSA_DH)
    out_sample = (y_s.reshape(db, 1, d), sample_rows(kvc_st), sample_rows(kvs_st), kv_rows(win_out),
                  s_s.reshape(1, db, GLA_HEADS, GLA_DK, GLA_DV), conv_s.reshape(1, db, CONV_W - 1, dff))
    return (out_prompt[0], out_sample[0]) + out_prompt[1:] + out_sample[1:]
```

```python
import functools

import numpy as np
import jax
import jax.numpy as jnp
from jax import lax
from jax.experimental import pallas as pl
from jax.experimental.pallas import tpu as pltpu

F32 = jnp.float32
BF16 = jnp.bfloat16

GLA_HEADS = 8
GLA_DK = 32
GLA_DV = 64
GLA_RANK = 16
GLA_TAU = 16.0
GLA_CHUNK = 64
NSA_HEADS = 8
NSA_G = 2
NSA_REP = 4
NSA_DH = 64
L_CMP = 32
L_SEL = 64
N_SEL = 16
WINDOW = 512
Q_BLOCK = 128
CONV_W = 3
EPS = 1e-6
NEG = -1e30
SLOPES = tuple(float(2.0 ** -(h + 1)) for h in range(NSA_HEADS))

GQ_W = GLA_HEADS * GLA_DK
GV_W = GLA_HEADS * GLA_DV
NQ_W = NSA_HEADS * NSA_DH
KV_W = 2 * NSA_G * NSA_DH
GATE_OFF = GLA_RANK
SEL_TILE = 256
VMEM_LIMIT = 56 * 1024 * 1024


def _nt(a, b):
    return lax.dot_general(a, b, (((1,), (1,)), ((), ())), preferred_element_type=F32)


def _tn(a, b):
    return lax.dot_general(a, b, (((0,), (0,)), ((), ())), preferred_element_type=F32)


def _dot(a, b):
    return jnp.dot(a, b, preferred_element_type=F32)


def _split3(x):
    hi = x.astype(BF16)
    r1 = x - hi.astype(F32)
    mid = r1.astype(BF16)
    lo = (r1 - mid.astype(F32)).astype(BF16)
    return hi, mid, lo


def _dot_exact_lhs(m_bf16, x):
    hi, mid, lo = _split3(x)
    return _dot(m_bf16, hi) + _dot(m_bf16, mid) + _dot(m_bf16, lo)


def _sigmoid(x):
    return 1.0 / (1.0 + jnp.exp(-x))


def _log_sigmoid(x):
    return jnp.minimum(x, 0.0) - jnp.log(1.0 + jnp.exp(-jnp.abs(x)))


def _rms(x, g):
    ms = jnp.mean(x * x, axis=-1, keepdims=True)
    return x * lax.rsqrt(ms + EPS) * g


def _const_spec(shape):
    nd = len(shape)
    return pl.BlockSpec(shape, lambda *_: (0,) * nd, pipeline_mode=pl.Buffered(1))


def _lane_halves(pair):
    return pair, pltpu.roll(pair, NSA_DH, 1)


def _proj_kernel(x_ref, g_ref, wm_ref, ws_ref, wa_ref, ba_ref, bg_ref,
                 gq_ref, gk_ref, gv_ref, gr_ref, glog_ref, kvc_ref, kvs_ref, kvw_ref, *attn_refs, prompt):
    tm = x_ref.shape[0]
    h = _rms(x_ref[...], g_ref[...]).astype(BF16)

    def mm(a, b):
        return _dot(h, wm_ref[:, a:b])

    o = 0
    gq_ref[...] = mm(o, o + GQ_W) * (GLA_DK ** -0.5); o += GQ_W
    gk_ref[...] = mm(o, o + GQ_W); o += GQ_W
    gv_ref[...] = mm(o, o + GV_W); o += GV_W
    gr_ref[...] = mm(o, o + GV_W); o += GV_W
    nq = mm(o, o + NQ_W) * (NSA_DH ** -0.5); o += NQ_W
    kvc = mm(o, o + KV_W); o += KV_W
    kvs = mm(o, o + KV_W); o += KV_W
    kvw = mm(o, o + KV_W); o += KV_W
    kvs_t = kvs.T
    kvw_t = kvw.T
    kvc_ref[0] = kvc.T
    kvs_ref[0] = kvs_t
    kvw_ref[0] = kvw_t
    small = _dot(h, ws_ref[...])
    z = _dot(small.astype(BF16), wa_ref[...]) + ba_ref[...]
    glog_ref[...] = _log_sigmoid(z) * (1.0 / GLA_TAU)
    gates = _sigmoid(small + bg_ref[...])
    if not prompt:
        nqh_ref, gates_ref, kvs_rows_ref, kvw_rows_ref = attn_refs
        for hh in range(NSA_HEADS):
            nqh_ref[0, hh] = nq[:, hh * NSA_DH:(hh + 1) * NSA_DH].astype(BF16)
        gates_ref[...] = gates
        kvs_rows_ref[...] = kvs
        kvw_rows_ref[...] = kvw
        return
    qa_ref, ksa_ref, kwa_ref, vst_ref, vwt_ref, cm_ref, gt_ref = attn_refs
    lane = lax.broadcasted_iota(jnp.int32, (tm, 128), 1)
    low = lane < NSA_DH
    for j in range(NSA_HEADS // 2):
        for hh, qh in zip((2 * j, 2 * j + 1), _lane_halves(nq[:, j * 128:(j + 1) * 128])):
            qa_ref[0, hh] = jnp.where(low, qh, jnp.where(lane == NSA_DH, SLOPES[hh], 0.0)).astype(BF16)
    pos = pl.program_id(1) * tm + lax.broadcasted_iota(jnp.int32, (tm, 128), 0)
    tile_pos = (pos % SEL_TILE).astype(F32)
    for src, src_t, ka_ref, vt_ref, tile in ((kvs, kvs_t, ksa_ref, vst_ref, SEL_TILE),
                                             (kvw, kvw_t, kwa_ref, vwt_ref, Q_BLOCK)):
        for g, kg in enumerate(_lane_halves(src[:, 0:NSA_G * NSA_DH])):
            ka_ref[0, g] = jnp.where(low, kg, tile_pos).astype(BF16)
        vt = src_t[NSA_G * NSA_DH:, :].astype(BF16)
        for g in range(NSA_G):
            for c in range(tm // tile):
                vt_ref[0, g, c] = vt[g * NSA_DH:(g + 1) * NSA_DH, c * tile:(c + 1) * tile]
    cm_ref[0] = kvc.reshape(tm // L_CMP, L_CMP, KV_W).sum(axis=1) * (1.0 / L_CMP)
    gt_ref[...] = gates.T


def _proj(x3, norm_g, wm, ws, wa, ba, bg, tm, prompt):
    b, t, d = x3.shape
    n = b * t
    nt = t // tm
    x2 = x3.reshape(n, d)
    row = lambda w: pl.BlockSpec((tm, w), lambda bi, ti: (bi * nt + ti, 0))
    sds = jax.ShapeDtypeStruct
    kvt = pl.BlockSpec((1, KV_W, tm), lambda bi, ti: (bi, 0, ti))
    out_shape = [sds((n, GQ_W), F32), sds((n, GQ_W), F32), sds((n, GV_W), F32), sds((n, GV_W), F32),
                 sds((n, GQ_W), F32), sds((b, KV_W, t), F32), sds((b, KV_W, t), F32), sds((b, KV_W, t), F32)]
    out_specs = [row(GQ_W), row(GQ_W), row(GV_W), row(GV_W), row(GQ_W), kvt, kvt, kvt]
    if prompt:
        hm = lambda nh: pl.BlockSpec((1, nh, tm, 128), lambda bi, ti: (bi, 0, ti, 0))
        vt = lambda tile: pl.BlockSpec((1, NSA_G, tm // tile, NSA_DH, tile), lambda bi, ti: (bi, 0, ti, 0, 0))
        out_shape += [sds((b, NSA_HEADS, t, 128), BF16), sds((b, NSA_G, t, 128), BF16), sds((b, NSA_G, t, 128), BF16),
                      sds((b, NSA_G, t // SEL_TILE, NSA_DH, SEL_TILE), BF16),
                      sds((b, NSA_G, t // Q_BLOCK, NSA_DH, Q_BLOCK), BF16),
                      sds((b, t // L_CMP, KV_W), F32), sds((128, n), F32)]
        out_specs += [hm(NSA_HEADS), hm(NSA_G), hm(NSA_G), vt(SEL_TILE), vt(Q_BLOCK),
                      pl.BlockSpec((1, tm // L_CMP, KV_W), lambda bi, ti: (bi, ti, 0)),
                      pl.BlockSpec((128, tm), lambda bi, ti: (0, bi * nt + ti))]
    else:
        out_shape += [sds((b, NSA_HEADS, t, NSA_DH), BF16), sds((n, 128), F32), sds((n, KV_W), F32), sds((n, KV_W), F32)]
        out_specs += [pl.BlockSpec((1, NSA_HEADS, tm, NSA_DH), lambda bi, ti: (bi, 0, ti, 0)), row(128),
                      row(KV_W), row(KV_W)]
    in_specs = [row(d), _const_spec((1, d)), _const_spec(wm.shape), _const_spec(ws.shape),
                _const_spec(wa.shape), _const_spec(ba.shape), _const_spec(bg.shape)]
    return pl.pallas_call(
        functools.partial(_proj_kernel, prompt=prompt), grid=(b, nt), in_specs=in_specs, out_specs=out_specs,
        out_shape=out_shape,
        compiler_params=pltpu.CompilerParams(dimension_semantics=("arbitrary", "arbitrary"),
                                             vmem_limit_bytes=VMEM_LIMIT),
        name="proj")(x2, norm_g, wm, ws, wa, ba, bg)


GLA_CHUNKS_PER_TRIP = 4


def _gla_consts():
    c = GLA_CHUNK
    t = np.arange(c)
    s = t[None, :]
    masks = []
    m = c // 2
    while m >= 1:
        blk = t // (2 * m)
        upper = t > blk * 2 * m + m - 1
        masks.append((upper[:, None] & (~upper)[None, :] & (blk[:, None] == blk[None, :])).astype(np.float32))
        m //= 2
    mcat = np.tril(np.ones((c, c), np.float32))
    lvl = np.stack([np.tile(mk, (1, GLA_HEADS)) for mk in masks])
    eye = np.tile(np.eye(c, dtype=np.float32), (1, GLA_HEADS))
    hk = np.arange(GLA_HEADS * GLA_DK) // GLA_DK
    hc = np.arange(GLA_HEADS * c) // c
    hv = np.arange(GLA_HEADS * GLA_DV) // GLA_DV
    bdk = (hc[:, None] == hk[None, :]).astype(np.float32)
    bdv = (hc[:, None] == hv[None, :]).astype(np.float32)
    bds = (hv[:, None] == hk[None, :]).astype(np.float32)
    hsum = (hk[:, None] == hc[None, :]).astype(np.float32)
    hmean = (hv[:, None] == hv[None, :]).astype(np.float32) / GLA_DV
    return (jnp.asarray(mcat, BF16), jnp.asarray(lvl, F32), jnp.asarray(eye, F32), jnp.asarray(bdk, BF16),
            jnp.asarray(bdv, BF16), jnp.asarray(bds, F32), jnp.asarray(hsum, BF16), jnp.asarray(hmean, BF16))


def _gla_kernel(q_ref, k_ref, v_ref, g_ref, r_ref, gain_ref, mcat_ref, lvl_ref, eye_ref, bdk_ref, bdv_ref,
                bds_ref, hsum_ref, hmean_ref, y_ref, st_ref, s_sc):
    c = GLA_CHUNK
    nlev = lvl_ref.shape[0]
    n_chunks = q_ref.shape[0] // c

    @pl.when(pl.program_id(1) == 0)
    def _():
        s_sc[...] = jnp.zeros_like(s_sc)

    t_row = lax.broadcasted_iota(jnp.int32, (c, 1), 0)
    halves = [c >> (lv + 1) for lv in range(nlev)]
    offs = [t_row - ((t_row // (2 * m)) * (2 * m) + m - 1) for m in halves]

    def ref_rows(b, lv):
        m = halves[lv]
        if 2 * m >= 16:
            return jnp.concatenate([jnp.broadcast_to(b[s0 + m - 1:s0 + m, :], (2 * m, b.shape[1]))
                                    for s0 in range(0, c, 2 * m)], axis=0)
        out = b
        for d in range(-(m - 1), m + 1):
            if d != 0:
                out = jnp.where(offs[lv] == d, pltpu.roll(b, d % c, 0), out)
        return out

    def chunk(ci):
        r0 = pl.multiple_of(ci * c, c)
        q = q_ref[pl.ds(r0, c), :]
        k = k_ref[pl.ds(r0, c), :]
        v = v_ref[pl.ds(r0, c), :].astype(BF16)
        g = g_ref[pl.ds(r0, c), :]
        b = _dot_exact_lhs(mcat_ref[...], g)
        a = _dot((q * k).astype(BF16), hsum_ref[...]) * eye_ref[...]
        for lv in range(nlev):
            delta = b - ref_rows(b, lv)
            upper = offs[lv] > 0
            qt = (q * jnp.exp(jnp.where(upper, delta, 0.0))).astype(BF16)
            kt = (k * jnp.exp(jnp.where(upper, 0.0, -delta))).astype(BF16)
            kbd = jnp.concatenate([kt] * GLA_HEADS, axis=0) * bdk_ref[...]
            a = a + _nt(qt, kbd) * lvl_ref[lv]
        vbd = jnp.concatenate([v] * GLA_HEADS, axis=0) * bdv_ref[...]
        s_t = s_sc[...]
        o = _dot(a.astype(BF16), vbd) + _nt((q * jnp.exp(b)).astype(BF16), s_t.astype(BF16))
        b_last = b[c - 1:c, :]
        kd = (k * jnp.exp(b_last - b)).astype(BF16)
        s_sc[...] = s_t * jnp.exp(b_last) + _tn(v, kd) * bds_ref[...]
        o2 = o * o
        o2h = o2.astype(BF16)
        o2l = (o2 - o2h.astype(F32)).astype(BF16)
        ms2 = _dot(jnp.concatenate([o2h, o2l], axis=0), hmean_ref[...])
        ms = ms2[0:c] + ms2[c:2 * c]
        r = r_ref[pl.ds(r0, c), :]
        y = o * lax.rsqrt(ms + EPS) * gain_ref[...] * (r * _sigmoid(r))
        y_ref[pl.ds(r0, c), :] = y.astype(BF16)

    def chunk_group(pi, carry):
        for j in range(GLA_CHUNKS_PER_TRIP):
            chunk(GLA_CHUNKS_PER_TRIP * pi + j)
        return carry

    lax.fori_loop(0, n_chunks // GLA_CHUNKS_PER_TRIP, chunk_group, 0)

    @pl.when(pl.program_id(1) == pl.num_programs(1) - 1)
    def _():
        st_ref[0] = s_sc[...]


def _gla_prompt(gq, gk, gv, glog, gr, gain, b, t, tc):
    n = b * t
    nt = t // tc
    consts = _gla_consts()
    gain_t = jnp.tile(gain.astype(F32).reshape(1, GLA_DV), (1, GLA_HEADS))
    row = lambda w: pl.BlockSpec((tc, w), lambda bi, ti: (bi * nt + ti, 0))
    in_specs = [row(GQ_W), row(GQ_W), row(GV_W), row(GQ_W), row(GV_W), _const_spec((1, GV_W))]
    in_specs += [_const_spec(a.shape) for a in consts]
    y, st = pl.pallas_call(
        _gla_kernel, grid=(b, nt), in_specs=in_specs,
        out_specs=[row(GV_W), pl.BlockSpec((1, GV_W, GQ_W), lambda bi, ti: (bi, 0, 0))],
        out_shape=[jax.ShapeDtypeStruct((n, GV_W), BF16), jax.ShapeDtypeStruct((b, GV_W, GQ_W), F32)],
        scratch_shapes=[pltpu.VMEM((GV_W, GQ_W), F32)],
        compiler_params=pltpu.CompilerParams(dimension_semantics=("arbitrary", "arbitrary"),
                                             vmem_limit_bytes=VMEM_LIMIT),
        name="gla_prompt")(gq, gk, gv, glog, gr, gain_t, *consts)
    s = jnp.stack([st[:, h * GLA_DV:(h + 1) * GLA_DV, h * GLA_DK:(h + 1) * GLA_DK] for h in range(GLA_HEADS)], axis=1)
    return y, s.transpose(0, 1, 3, 2)


def _gla_step_kernel(q_ref, k_ref, g_ref, v_ref, r_ref, gain_ref, s0_ref, y_ref, s1_ref):
    bb = q_ref.shape[0]
    hd = GLA_HEADS * GLA_DK
    ri = lax.broadcasted_iota(jnp.int32, (hd, hd), 0)
    ci = lax.broadcasted_iota(jnp.int32, (hd, hd), 1)
    eye = ri == ci
    rowhead = lax.broadcasted_iota(jnp.int32, (hd, GLA_DV), 0) // GLA_DK

    def col(x_row):
        return jnp.sum(jnp.where(eye, jnp.broadcast_to(x_row, (hd, hd)), 0.0), axis=1, keepdims=True)

    for i in range(bb):
        q = q_ref[i:i + 1, :]
        k = k_ref[i:i + 1, :]
        eg = jnp.exp(g_ref[i:i + 1, :])
        v3 = v_ref[i]
        s0 = s0_ref[i]
        vrows = jnp.zeros((hd, GLA_DV), F32)
        for h in range(GLA_HEADS):
            vrows = jnp.where(rowhead == h, jnp.broadcast_to(v3[h:h + 1, :], (hd, GLA_DV)), vrows)
        s1_ref[i] = col(eg) * s0 + col(k) * vrows
        t1 = (col(q * eg) * s0).reshape(GLA_HEADS, GLA_DK, GLA_DV).sum(axis=1)
        qk = col(q * k).reshape(GLA_HEADS, GLA_DK, 1).sum(axis=1)
        o = t1 + qk * v3
        ms = jnp.mean(o * o, axis=-1, keepdims=True)
        r3 = r_ref[i]
        y_ref[i] = (o * lax.rsqrt(ms + EPS) * gain_ref[...] * (r3 * _sigmoid(r3))).astype(BF16)


def _gla_step(gq, gk, glog, gv, gr, gain, s0, bb=8):
    n = gq.shape[0]
    hd = GLA_HEADS * GLA_DK
    v3 = gv.reshape(n, GLA_HEADS, GLA_DV)
    r3 = gr.reshape(n, GLA_HEADS, GLA_DV)
    s0r = s0.reshape(n, hd, GLA_DV)
    row = pl.BlockSpec((bb, hd), lambda i: (i, 0))
    h3 = pl.BlockSpec((bb, GLA_HEADS, GLA_DV), lambda i: (i, 0, 0))
    st = pl.BlockSpec((bb, hd, GLA_DV), lambda i: (i, 0, 0))
    y, s1 = pl.pallas_call(
        _gla_step_kernel, grid=(n // bb,),
        in_specs=[row, row, row, h3, h3, _const_spec((1, GLA_DV)), st],
        out_specs=[h3, st],
        out_shape=[jax.ShapeDtypeStruct((n, GLA_HEADS, GLA_DV), BF16), jax.ShapeDtypeStruct((n, hd, GLA_DV), F32)],
        compiler_params=pltpu.CompilerParams(dimension_semantics=("arbitrary",)),
        name="gla_step")(gq, gk, glog, v3, r3, gain.astype(F32).reshape(1, GLA_DV), s0r)
    return y.reshape(n, GV_W), s1


def _topk_mask_t(v, k_sel):
    n = v.shape[0]
    sb = lax.broadcasted_iota(jnp.int32, v.shape, 0)
    sel = jnp.zeros(v.shape, F32)
    for _ in range(k_sel):
        m = jnp.max(v, axis=0, keepdims=True)
        idx = jnp.min(jnp.where(v == m, sb, n), axis=0, keepdims=True)
        hit = sb == idx
        sel = jnp.where(hit, 1.0, sel)
        v = jnp.where(hit, -jnp.inf, v)
    return sel


def _topk_mask(v, k_sel):
    n = v.shape[-1]
    sb = lax.broadcasted_iota(jnp.int32, v.shape, 1)
    sel = jnp.zeros(v.shape, F32)
    idxs = []
    for _ in range(k_sel):
        m = jnp.max(v, axis=-1, keepdims=True)
        idx = jnp.min(jnp.where(v == m, sb, n), axis=-1, keepdims=True)
        hit = sb == idx
        sel = jnp.where(hit, 1.0, sel)
        v = jnp.where(hit, -jnp.inf, v)
        idxs.append(idx)
    return sel, idxs


def _group_queries(q_ref, g):
    return jnp.concatenate([q_ref[0, g * NSA_REP + r] for r in range(NSA_REP)], axis=0)


def _nsa_cmp_kernel(q_ref, cm_ref, oc_ref, selm_ref, any_ref):
    nc = cm_ref.shape[1]
    half = nc // 2
    ns = half
    q0 = pl.program_id(1) * Q_BLOCK
    qpos = q0 + lax.broadcasted_iota(jnp.int32, (1, Q_BLOCK), 1)
    row = lax.broadcasted_iota(jnp.int32, (nc, 1), 0)
    blk = jnp.where(row < half, 2 * row, 2 * (row - half) + 1)
    mask_c = ((blk + 1) * L_CMP - 1) <= qpos
    dist_c = qpos.astype(F32) - (blk.astype(F32) * L_CMP + (L_CMP - 1) / 2.0)
    sb = lax.broadcasted_iota(jnp.int32, (ns, 1), 0)
    cur = qpos // L_SEL
    forced = (sb == 0) | (sb == cur) | (sb == cur - 1)
    future = sb * L_SEL > qpos
    cm = cm_ref[0]
    low = lax.broadcasted_iota(jnp.int32, (nc, 128), 1) < NSA_DH
    vt_all = cm[:, NSA_G * NSA_DH:].T
    ones = jnp.ones((8, Q_BLOCK), BF16)
    for g, kg in enumerate(_lane_halves(cm[:, 0:NSA_G * NSA_DH])):
        kc = jnp.where(low, kg, 0.0).astype(BF16)
        vct = vt_all[g * NSA_DH:(g + 1) * NSA_DH, :].astype(BF16)
        s_all = _nt(kc, _group_queries(q_ref, g))
        imp = jnp.zeros((nc, Q_BLOCK), F32)
        ps = []
        for r in range(NSA_REP):
            h = g * NSA_REP + r
            s = s_all[:, r * Q_BLOCK:(r + 1) * Q_BLOCK] - SLOPES[h] * dist_c
            s = jnp.where(mask_c, s, NEG)
            e = jnp.exp(s - jnp.max(s, axis=0, keepdims=True))
            p = jnp.where(mask_c, e * (1.0 / jnp.sum(e, axis=0, keepdims=True)), 0.0)
            ps.append(p.astype(BF16))
            imp = imp + p
        oct = _dot(vct, jnp.concatenate(ps, axis=1))
        for r in range(NSA_REP):
            oc_ref[0, g * NSA_REP + r] = oct[:, r * Q_BLOCK:(r + 1) * Q_BLOCK]
        imp2 = imp[:half] + imp[half:]
        v = jnp.where(future, -jnp.inf, jnp.where(forced, jnp.inf, imp2))
        sel = _topk_mask_t(v, min(N_SEL, ns))
        selm_ref[0, g] = sel
        any_ref[0, g:g + 1, :] = _nt(ones, sel.astype(BF16))[0:1]


def _nsa_cmp(qa, cm_perm):
    b, _, t, _ = qa.shape
    nqb = t // Q_BLOCK
    nc = t // L_CMP
    ns = t // L_SEL
    return pl.pallas_call(
        _nsa_cmp_kernel, grid=(b, nqb),
        in_specs=[pl.BlockSpec((1, NSA_HEADS, Q_BLOCK, 128), lambda bi, i: (bi, 0, i, 0)),
                  pl.BlockSpec((1, nc, KV_W), lambda bi, i: (bi, 0, 0))],
        out_specs=[pl.BlockSpec((1, NSA_HEADS, NSA_DH, Q_BLOCK), lambda bi, i: (bi, 0, 0, i)),
                   pl.BlockSpec((1, NSA_G, ns, Q_BLOCK), lambda bi, i: (bi, 0, 0, i)),
                   pl.BlockSpec((1, NSA_G, ns), lambda bi, i: (bi * nqb + i, 0, 0))],
        out_shape=[jax.ShapeDtypeStruct((b, NSA_HEADS, NSA_DH, t), F32),
                   jax.ShapeDtypeStruct((b, NSA_G, ns, t), F32),
                   jax.ShapeDtypeStruct((b * nqb, NSA_G, ns), F32)],
        compiler_params=pltpu.CompilerParams(dimension_semantics=("arbitrary", "arbitrary"),
                                             vmem_limit_bytes=VMEM_LIMIT),
        name="nsa_cmp")(qa, cm_perm)


WIN_TILES = (WINDOW + Q_BLOCK) // Q_BLOCK


def _nsa_sel_kernel(bits_ref, q_ref, ksa_ref, kwa_ref, vst_ref, vwt_ref, selm_ref, oc_ref, gt_ref, y_ref,
                    m_sc, l_sc, acc_sc):
    t_len = ksa_ref.shape[2]
    nqb = pl.num_programs(1)
    bi = pl.program_id(0)
    i = pl.program_id(1)
    q0 = i * Q_BLOCK
    qpos = q0 + lax.broadcasted_iota(jnp.int32, (1, Q_BLOCK), 1)
    diag = q0 // SEL_TILE
    nwords = (t_len // SEL_TILE + 15) // 16
    per = SEL_TILE // L_SEL
    wt0 = jnp.clip(i - WINDOW // Q_BLOCK, 0, t_len // Q_BLOCK - WIN_TILES)
    wstart = pl.multiple_of(wt0 * Q_BLOCK, Q_BLOCK)
    wkeys = WIN_TILES * Q_BLOCK
    wdist = qpos - (wstart + lax.broadcasted_iota(jnp.int32, (wkeys, 1), 0))
    wvalid = (wdist >= 0) & (wdist <= WINDOW)
    kiota = lax.broadcasted_iota(jnp.int32, (SEL_TILE, 1), 0)
    gts = gt_ref[...]
    q_alls = [_group_queries(q_ref, g) for g in range(NSA_G)]
    m_sc[...] = jnp.full(m_sc.shape, NEG, F32)
    l_sc[...] = jnp.zeros_like(l_sc)
    acc_sc[...] = jnp.zeros_like(acc_sc)
    base = (bi * nqb + i) * NSA_G * nwords

    def attend_tile(g, t, causal):
        ks = pl.multiple_of(t * SEL_TILE, SEL_TILE)
        s_all = _nt(ksa_ref[0, g, pl.ds(ks, SEL_TILE), :], q_alls[g])
        mk = jnp.concatenate([jnp.broadcast_to(selm_ref[0, g, pl.ds(t * per + j, 1), :], (L_SEL, Q_BLOCK))
                              for j in range(per)], axis=0) > 0.5
        if causal:
            mk = mk & ((ks + kiota) <= qpos)
        shift = (ks - q0).astype(F32)
        ps, alphas = [], []
        for r in range(NSA_REP):
            cs = slice(r * Q_BLOCK, (r + 1) * Q_BLOCK)
            s = jnp.where(mk, s_all[:, cs] + SLOPES[g * NSA_REP + r] * shift, NEG)
            m_old = m_sc[g, :, cs]
            m_new = jnp.maximum(m_old, jnp.max(s, axis=0, keepdims=True))
            alpha = jnp.exp(m_old - m_new)
            p = jnp.exp(s - m_new)
            l_sc[g, :, cs] = alpha * l_sc[g, :, cs] + jnp.sum(p, axis=0, keepdims=True)
            m_sc[g, :, cs] = m_new
            ps.append(p.astype(BF16))
            alphas.append(alpha)
        acc_sc[g] = jnp.concatenate(alphas, axis=1) * acc_sc[g] + _dot(vst_ref[0, g, t], jnp.concatenate(ps, axis=1))

    def tile(t, carry):
        flag = bits_ref[base + t // 16]
        for g in range(1, NSA_G):
            flag = flag | bits_ref[base + g * nwords + t // 16]

        @pl.when((lax.shift_right_logical(flag, t % 16) & 1) == 1)
        def _():
            for g in range(NSA_G):
                attend_tile(g, t, False)

        return carry

    lax.fori_loop(0, diag, tile, 0)
    for g in range(NSA_G):
        attend_tile(g, diag, True)

    for g in range(NSA_G):
        q_all = q_alls[g]
        o_s = acc_sc[g] * (1.0 / l_sc[g])
        s_all = _nt(kwa_ref[0, g, pl.ds(wstart, wkeys), :], q_all)
        vw = jnp.concatenate([vwt_ref[0, g, wt0 + j] for j in range(WIN_TILES)], axis=1)
        shifts = [(((wstart + j * Q_BLOCK) // SEL_TILE) * SEL_TILE - q0).astype(F32) for j in range(WIN_TILES)]
        ps = []
        for r in range(NSA_REP):
            slope = SLOPES[g * NSA_REP + r]
            s = jnp.concatenate([s_all[j * Q_BLOCK:(j + 1) * Q_BLOCK, r * Q_BLOCK:(r + 1) * Q_BLOCK] + slope * shifts[j]
                                 for j in range(WIN_TILES)], axis=0)
            s = jnp.where(wvalid, s, NEG)
            e = jnp.exp(s - jnp.max(s, axis=0, keepdims=True))
            ps.append((e * (1.0 / jnp.sum(e, axis=0, keepdims=True))).astype(BF16))
        o_w = _dot(vw, jnp.concatenate(ps, axis=1))

        outs = []
        for r in range(NSA_REP):
            h = g * NSA_REP + r
            c0 = GATE_OFF + h * 3
            cs = slice(r * Q_BLOCK, (r + 1) * Q_BLOCK)
            outs.append(gts[c0:c0 + 1, :] * oc_ref[0, h] + gts[c0 + 1:c0 + 2, :] * o_s[:, cs]
                        + gts[c0 + 2:c0 + 3, :] * o_w[:, cs])
        gw = NSA_REP * NSA_DH
        y_ref[:, g * gw:(g + 1) * gw] = jnp.concatenate(outs, axis=0).T.astype(BF16)


def _nsa_sel(bits, qa, ksa, kwa, vst, vwt, selm, oc, gates_t):
    b, _, t, _ = qa.shape
    nqb = t // Q_BLOCK
    ns = t // L_SEL
    whole = lambda a: pl.BlockSpec((1,) + a.shape[1:], lambda bi, i, *_: (bi,) + (0,) * (a.ndim - 1),
                                   pipeline_mode=pl.Buffered(1))
    cols = lambda lead: pl.BlockSpec((1,) + lead + (Q_BLOCK,), lambda bi, i, *_: (bi,) + (0,) * len(lead) + (i,))
    grid_spec = pltpu.PrefetchScalarGridSpec(
        num_scalar_prefetch=1, grid=(b, nqb),
        in_specs=[pl.BlockSpec((1, NSA_HEADS, Q_BLOCK, 128), lambda bi, i, *_: (bi, 0, i, 0)),
                  whole(ksa), whole(kwa), whole(vst), whole(vwt),
                  cols((NSA_G, ns)), cols((NSA_HEADS, NSA_DH)),
                  pl.BlockSpec((128, Q_BLOCK), lambda bi, i, *_: (0, bi * nqb + i))],
        out_specs=[pl.BlockSpec((Q_BLOCK, NQ_W), lambda bi, i, *_: (bi * nqb + i, 0))],
        scratch_shapes=[pltpu.VMEM((NSA_G, 1, NSA_REP * Q_BLOCK), F32), pltpu.VMEM((NSA_G, 1, NSA_REP * Q_BLOCK), F32),
                        pltpu.VMEM((NSA_G, NSA_DH, NSA_REP * Q_BLOCK), F32)])
    return pl.pallas_call(
        _nsa_sel_kernel, grid_spec=grid_spec,
        out_shape=[jax.ShapeDtypeStruct((b * t, NQ_W), BF16)],
        compiler_params=pltpu.CompilerParams(dimension_semantics=("arbitrary", "arbitrary"),
                                             vmem_limit_bytes=VMEM_LIMIT),
        name="nsa_sel")(bits, qa, ksa, kwa, vst, vwt, selm, oc, gates_t)[0]


def _tile_bits(anyf, ns):
    r = anyf.shape[0]
    per = SEL_TILE // L_SEL
    ntile = ns // per
    nwords = (ntile + 15) // 16
    tf = (anyf.reshape(r, NSA_G, ntile, per).max(axis=-1) > 0.5).astype(jnp.int32)
    tf = jnp.pad(tf, ((0, 0), (0, 0), (0, nwords * 16 - ntile))).reshape(r, NSA_G, nwords, 16)
    return jnp.sum(tf << jnp.arange(16, dtype=jnp.int32), axis=-1).reshape(-1).astype(jnp.int32)


PAGES_PER_STEP = 8


def _page_mean_kernel(pt_ref, cache_ref, avg_ref, o_ref, buf, sems, *, npg, steps_per_tile):
    bi = pl.program_id(0)
    n_chunks = npg // PAGES_PER_STEP
    psz = cache_ref.shape[-1]
    cols = avg_ref.shape[-1]

    def page_copy(seq, j):
        return pltpu.make_async_copy(cache_ref.at[pt_ref[seq * npg + j]], buf.at[seq % 2, j], sems.at[seq % 2, j])

    def start_sequence(seq):
        for j in range(npg):
            page_copy(seq, j).start()

    @pl.when(bi == 0)
    def _():
        start_sequence(bi)

    @pl.when(bi + 1 < pl.num_programs(0))
    def _():
        start_sequence(bi + 1)

    slot = bi % 2
    acc = None
    for c in range(n_chunks):
        for p in range(PAGES_PER_STEP):
            page_copy(bi, c * PAGES_PER_STEP + p).wait()
        x = jnp.concatenate([buf[slot, c * PAGES_PER_STEP + p].reshape(KV_W, psz) for p in range(PAGES_PER_STEP)],
                            axis=1)
        part = _dot(x.astype(BF16), avg_ref[c % steps_per_tile])
        acc = part if c % steps_per_tile == 0 else acc + part
        if c % steps_per_tile == steps_per_tile - 1:
            tile = c // steps_per_tile
            o_ref[0, :, tile * cols:(tile + 1) * cols] = acc


def _page_means(cache_t, page_table):
    db, npg = page_table.shape
    psz = cache_t.shape[-1]
    per = psz // L_CMP
    nblk = npg * per
    cols = min(128, nblk)
    step_blocks = PAGES_PER_STEP * per
    steps_per_tile = cols // step_blocks
    n_chunks = npg // PAGES_PER_STEP
    assert npg % PAGES_PER_STEP == 0 and n_chunks % steps_per_tile == 0
    tok = np.arange(PAGES_PER_STEP * psz)
    avg = np.zeros((steps_per_tile, PAGES_PER_STEP * psz, cols), np.float32)
    for s in range(steps_per_tile):
        avg[s, tok, s * step_blocks + tok // L_CMP] = 1.0 / L_CMP

    grid_spec = pltpu.PrefetchScalarGridSpec(
        num_scalar_prefetch=1, grid=(db,),
        in_specs=[pl.BlockSpec(memory_space=pl.ANY),
                  pl.BlockSpec(avg.shape, lambda bi, pt: (0, 0, 0), pipeline_mode=pl.Buffered(1))],
        out_specs=[pl.BlockSpec((1, KV_W, nblk), lambda bi, pt: (bi, 0, 0))],
        scratch_shapes=[pltpu.VMEM((2, npg) + cache_t.shape[1:], F32), pltpu.SemaphoreType.DMA((2, npg))])
    return pl.pallas_call(
        functools.partial(_page_mean_kernel, npg=npg, steps_per_tile=steps_per_tile), grid_spec=grid_spec,
        out_shape=[jax.ShapeDtypeStruct((db, KV_W, nblk), F32)],
        compiler_params=pltpu.CompilerParams(dimension_semantics=("arbitrary",), vmem_limit_bytes=VMEM_LIMIT),
        name="page_means")(page_table.reshape(-1), cache_t, jnp.asarray(avg, BF16))[0]


def _slope_col():
    row = lax.broadcasted_iota(jnp.int32, (NSA_HEADS, 1), 0)
    sl = jnp.zeros((NSA_HEADS, 1), F32)
    for h in range(NSA_HEADS):
        sl = jnp.where(row == h, SLOPES[h], sl)
    return row, sl


def _nsa_cmp_step_kernel(q_ref, cm_ref, oc_ref, idx_ref, *, past, k_sel):
    bb = q_ref.shape[0]
    nc = cm_ref.shape[2]
    half = nc // 2
    blk = lax.broadcasted_iota(jnp.int32, (1, nc), 1)
    pair = jnp.where(lax.broadcasted_iota(jnp.int32, (nc, half), 0) // (L_SEL // L_CMP)
                     == lax.broadcasted_iota(jnp.int32, (nc, half), 1), 1.0, 0.0).astype(BF16)
    mask_c = ((blk + 1) * L_CMP - 1) <= past
    dist_c = float(past) - (blk.astype(F32) * L_CMP + (L_CMP - 1) / 2.0)
    row, sl = _slope_col()
    imps = []
    for i in range(bb):
        q8 = q_ref[i]
        oc = jnp.zeros((NSA_HEADS, NSA_DH), F32)
        for g in range(NSA_G):
            kct = cm_ref[i, g * NSA_DH:(g + 1) * NSA_DH, :].astype(BF16)
            vct = cm_ref[i, (NSA_G + g) * NSA_DH:(NSA_G + g + 1) * NSA_DH, :].astype(BF16)
            s = _dot(q8, kct) - sl * dist_c
            s = jnp.where(mask_c, s, NEG)
            e = jnp.exp(s - jnp.max(s, axis=-1, keepdims=True))
            p = jnp.where(mask_c, e / jnp.sum(e, axis=-1, keepdims=True), 0.0)
            ingroup = (row // NSA_REP) == g
            oc = jnp.where(ingroup, _nt(p.astype(BF16), vct), oc)
            hi, mid, lo = _split3(jnp.where(ingroup, p, 0.0))
            imps.append(jnp.sum(_dot(hi, pair) + _dot(mid, pair) + _dot(lo, pair), axis=0, keepdims=True))
        oc_ref[i] = oc
    imp = jnp.concatenate(imps, axis=0)
    sb = lax.broadcasted_iota(jnp.int32, imp.shape, 1)
    forced = (sb == 0) | (sb == half - 1)
    _, idxs = _topk_mask(jnp.where(forced, jnp.inf, imp), k_sel)
    lane = lax.broadcasted_iota(jnp.int32, (bb * NSA_G, k_sel), 1)
    acc = jnp.zeros((bb * NSA_G, k_sel), jnp.int32)
    for t, ix in enumerate(idxs):
        acc = jnp.where(lane == t, ix, acc)
    for i in range(bb):
        idx_ref[i] = acc[i * NSA_G:(i + 1) * NSA_G, :]


def _nsa_cmp_step(q3, cm_s, past, k_sel, bb=8):
    db = q3.shape[0]
    nc = cm_s.shape[2]
    return pl.pallas_call(
        functools.partial(_nsa_cmp_step_kernel, past=past, k_sel=k_sel), grid=(db // bb,),
        in_specs=[pl.BlockSpec((bb, NSA_HEADS, NSA_DH), lambda bi: (bi, 0, 0)),
                  pl.BlockSpec((bb, KV_W, nc), lambda bi: (bi, 0, 0))],
        out_specs=[pl.BlockSpec((bb, NSA_HEADS, NSA_DH), lambda bi: (bi, 0, 0)),
                   pl.BlockSpec((bb, NSA_G, k_sel), lambda bi: (bi, 0, 0))],
        out_shape=[jax.ShapeDtypeStruct((db, NSA_HEADS, NSA_DH), F32),
                   jax.ShapeDtypeStruct((db, NSA_G, k_sel), jnp.int32)],
        compiler_params=pltpu.CompilerParams(dimension_semantics=("arbitrary",), vmem_limit_bytes=VMEM_LIMIT),
        name="nsa_cmp_step")(q3, cm_s)


def _nsa_sel_step_kernel(pt_ref, ix_ref, q_ref, oc_ref, gates_ref, ks_new_ref, kw_new_ref, kwt_ref, win_ref, cache_ref,
                         y_ref, wout_ref, blocks, sems, *, past, k_sel, npg, per_page):
    bi = pl.program_id(0)
    w_len = win_ref.shape[2]
    psz = cache_ref.shape[-1]
    n_sel = NSA_G * k_sel

    def page_copy(seq, a, j):
        n = ix_ref[seq * n_sel + j]
        page = pt_ref[seq * npg + n // per_page]
        return pltpu.make_async_copy(cache_ref.at[page, a, j // k_sel], blocks.at[seq % 2, a, j],
                                     sems.at[seq % 2, a, j])

    def start_gather(seq):
        for a in range(2):
            for j in range(n_sel):
                page_copy(seq, a, j).start()

    @pl.when(bi == 0)
    def _():
        start_gather(bi)

    @pl.when(bi + 1 < pl.num_programs(0))
    def _():
        start_gather(bi + 1)

    slot = bi % 2
    row, sl = _slope_col()
    q8 = q_ref[0]
    q8f = q8.astype(F32)
    gts = gates_ref[0]
    gcol = []
    for c in range(3):
        col = jnp.zeros((NSA_HEADS, 1), F32)
        for h in range(NSA_HEADS):
            c0 = GATE_OFF + h * 3 + c
            col = jnp.where(row == h, gts[:, c0:c0 + 1], col)
        gcol.append(col)
    win = win_ref[0]
    ks_new = ks_new_ref[0]
    kw_new = kw_new_ref[0]
    wpos = (past - w_len) + lax.broadcasted_iota(jnp.int32, (1, w_len), 1)
    wdist = past - wpos
    wvalid = (wdist <= WINDOW) & (wpos >= 0)

    def joint_attend(s, valid, k_new, v_new, v_t):
        s_new = jnp.sum(q8f * k_new.astype(BF16).astype(F32), axis=-1, keepdims=True)
        s = jnp.where(valid, s, NEG)
        m = jnp.maximum(jnp.max(s, axis=-1, keepdims=True), s_new)
        e = jnp.where(valid, jnp.exp(s - m), 0.0)
        en = jnp.exp(s_new - m)
        l = jnp.sum(e, axis=-1, keepdims=True) + en
        return _nt((e / l).astype(BF16), v_t) + (en / l).astype(BF16).astype(F32) * v_new.astype(BF16).astype(F32)

    ksl = [slice(g * NSA_DH, (g + 1) * NSA_DH) for g in range(NSA_G)]
    vsl = [slice((NSA_G + g) * NSA_DH, (NSA_G + g + 1) * NSA_DH) for g in range(NSA_G)]
    o_w = []
    for g in range(NSA_G):
        s = _dot(q8, win[ksl[g], :].astype(BF16)) - sl * wdist.astype(F32)
        o_w.append(joint_attend(s, wvalid, kw_new[:, ksl[g]], kw_new[:, vsl[g]], win[vsl[g], :].astype(BF16)))
    kwt = kwt_ref[...]
    new_col = jnp.sum(jnp.where(lax.broadcasted_iota(jnp.int32, kwt.shape, 1) == bi, kwt, 0.0), axis=1, keepdims=True)
    wlane = lax.broadcasted_iota(jnp.int32, (1, w_len), 1)
    wout_ref[0] = jnp.where(wlane == w_len - 1, new_col, pltpu.roll(win, w_len - 1, 1))
    for a in range(2):
        for j in range(n_sel):
            page_copy(bi, a, j).wait()
    lane = lax.broadcasted_iota(jnp.int32, (1, k_sel * psz), 1)
    within = lane % psz
    o = jnp.zeros((NSA_HEADS, NSA_DH), F32)
    for g in range(NSA_G):
        k_t = jnp.concatenate([blocks[slot, 0, g * k_sel + t] for t in range(k_sel)], axis=1).astype(BF16)
        v_t = jnp.concatenate([blocks[slot, 1, g * k_sel + t] for t in range(k_sel)], axis=1).astype(BF16)
        page_pos = jnp.zeros_like(lane)
        half = jnp.zeros_like(lane)
        for t in range(k_sel):
            n = ix_ref[bi * n_sel + g * k_sel + t]
            here = lane // psz == t
            page_pos = jnp.where(here, (n // per_page) * psz, page_pos)
            half = jnp.where(here, n % per_page, half)
        dist = past - (page_pos + within)
        valid = (within // L_SEL == half) & (dist >= 0)
        s = _dot(q8, k_t) - sl * dist.astype(F32)
        o_s = joint_attend(s, valid, ks_new[:, ksl[g]], ks_new[:, vsl[g]], v_t)
        og = gcol[0] * oc_ref[0] + gcol[1] * o_s + gcol[2] * o_w[g]
        o = jnp.where((row // NSA_REP) == g, og, o)
    y_ref[0] = o.astype(BF16)


def _nsa_sel_step(page_table, idx, q3, oc, gates3, ks_new3, kw_new3, kw_new_t, win_t, cache_t, past, k_sel):
    db, npg = page_table.shape
    w_len = win_t.shape[2]
    psz = cache_t.shape[-1]
    per_page = psz // L_SEL

    b3 = lambda w: pl.BlockSpec((1, 1, w), lambda bi, *_: (bi, 0, 0))
    h3 = pl.BlockSpec((1, NSA_HEADS, NSA_DH), lambda bi, *_: (bi, 0, 0))
    wsp = pl.BlockSpec((1, KV_W, w_len), lambda bi, *_: (bi, 0, 0))
    grid_spec = pltpu.PrefetchScalarGridSpec(
        num_scalar_prefetch=2, grid=(db,),
        in_specs=[h3, h3, b3(128), b3(KV_W), b3(KV_W),
                  pl.BlockSpec(kw_new_t.shape, lambda bi, *_: (0, 0), pipeline_mode=pl.Buffered(1)),
                  wsp, pl.BlockSpec(memory_space=pl.ANY)],
        out_specs=[h3, wsp],
        scratch_shapes=[pltpu.VMEM((2, 2, NSA_G * k_sel, NSA_DH, psz), F32),
                        pltpu.SemaphoreType.DMA((2, 2, NSA_G * k_sel))])
    return pl.pallas_call(
        functools.partial(_nsa_sel_step_kernel, past=past, k_sel=k_sel, npg=npg, per_page=per_page),
        grid_spec=grid_spec,
        out_shape=[jax.ShapeDtypeStruct((db, NSA_HEADS, NSA_DH), BF16), jax.ShapeDtypeStruct((db, KV_W, w_len), F32)],
        compiler_params=pltpu.CompilerParams(dimension_semantics=("arbitrary",), vmem_limit_bytes=VMEM_LIMIT),
        name="nsa_sel_step")(page_table.reshape(-1), idx.reshape(-1), q3, oc, gates3, ks_new3, kw_new3, kw_new_t,
                             win_t, cache_t)


FFN_CHUNK = 256


def _gelu(x):
    return 0.5 * x * (1.0 + lax.erf(x * (2.0 ** -0.5)))


def _ffn_kernel(*refs, seq_mode, tiles_per_seq):
    if seq_mode:
        (x_ref, yg_ref, yn_ref, wo_ref, nf_ref, wup_ref, cw_ref, cb_ref, wdn_ref, nl_ref,
         y_ref, conv_ref, abuf) = refs
    else:
        (x_ref, yg_ref, yn_ref, wo_ref, nf_ref, wup_ref, cw_ref, cb_ref, wdn_ref, nl_ref, prev_ref,
         y_ref, conv_ref) = refs
    tm = x_ref.shape[0]
    dff = wdn_ref.shape[0]
    gw = yg_ref.shape[1]
    x1 = x_ref[...] + _dot(yg_ref[...], wo_ref[0:gw, :]) + _dot(yn_ref[...], wo_ref[gw:, :])
    h2 = _rms(x1, nf_ref[...]).astype(BF16)
    if seq_mode:
        t_in = pl.program_id(0) % tiles_per_seq

        @pl.when(t_in == 0)
        def _():
            abuf[0:8, :] = jnp.zeros((8, dff), F32)

    acc = jnp.zeros(x1.shape, F32)
    for c0 in range(0, dff, FFN_CHUNK):
        cs = slice(c0, c0 + FFN_CHUNK)
        a = _dot(h2, wup_ref[:, cs])
        bgate = _dot(h2, wup_ref[:, dff + c0:dff + c0 + FFN_CHUNK])
        if seq_mode:
            abuf[8:8 + tm, cs] = a
            am1 = abuf[7:7 + tm, cs]
            am2 = abuf[6:6 + tm, cs]
        else:
            am2 = prev_ref[:, cs]
            am1 = prev_ref[:, dff + c0:dff + c0 + FFN_CHUNK]
            conv_ref[:, cs] = am1
            conv_ref[:, dff + c0:dff + c0 + FFN_CHUNK] = a
        conv = cb_ref[:, cs] + cw_ref[0:1, cs] * am2 + cw_ref[1:2, cs] * am1 + cw_ref[2:3, cs] * a
        act = (_gelu(conv) * bgate).astype(BF16)
        acc = acc + _dot(act, wdn_ref[cs, :])
    if seq_mode:
        @pl.when(t_in == tiles_per_seq - 1)
        def _():
            conv_ref[0] = abuf[8 + tm - (CONV_W - 1):8 + tm, :]

        abuf[0:8, :] = abuf[tm:tm + 8, :]
    y_ref[...] = _rms(x1 + acc, nl_ref[...])


def _ffn(x2, yg, yn, wo, nf, wup, cw, cb, wdn, nl, prev, b, t, tm):
    n, d = x2.shape
    dff = wdn.shape[0]
    seq_mode = prev is None
    row = lambda w: pl.BlockSpec((tm, w), lambda i: (i, 0))
    in_specs = [row(d), row(yg.shape[1]), row(yn.shape[1]), _const_spec(wo.shape), _const_spec((1, d)),
                _const_spec(wup.shape), _const_spec(cw.shape), _const_spec((1, dff)), _const_spec(wdn.shape),
                _const_spec((1, d))]
    args = [x2, yg, yn, wo, nf, wup, cw, cb, wdn, nl]
    if seq_mode:
        tps = t // tm
        conv_shape = jax.ShapeDtypeStruct((b, CONV_W - 1, dff), F32)
        conv_spec = pl.BlockSpec((1, CONV_W - 1, dff), lambda i: (i // tps, 0, 0))
        scratch = [pltpu.VMEM((8 + tm, dff), F32)]
    else:
        tps = 1
        in_specs.append(row(2 * dff))
        args.append(prev)
        conv_shape = jax.ShapeDtypeStruct((n, 2 * dff), F32)
        conv_spec = row(2 * dff)
        scratch = []
    return pl.pallas_call(
        functools.partial(_ffn_kernel, seq_mode=seq_mode, tiles_per_seq=tps), grid=(n // tm,),
        in_specs=in_specs, out_specs=[row(d), conv_spec],
        out_shape=[jax.ShapeDtypeStruct((n, d), F32), conv_shape], scratch_shapes=scratch,
        compiler_params=pltpu.CompilerParams(dimension_semantics=("arbitrary",), vmem_limit_bytes=VMEM_LIMIT),
        name="ffn_seq" if seq_mode else "ffn_step")(*args)


def _pack_in_weights(w_in, w_gla_a, b_gla_a, b_nsa_gate):
    sizes = (GQ_W, GQ_W, GV_W, GLA_RANK, GV_W, NQ_W, KV_W, KV_W, KV_W, 3 * NSA_HEADS)
    offs = np.concatenate([[0], np.cumsum(sizes)])
    part = lambda i: w_in[:, offs[i]:offs[i + 1]]
    wm = jnp.concatenate([part(0), part(1), part(2), part(4), part(5), part(6), part(7), part(8)], axis=1).astype(BF16)
    small = jnp.concatenate([part(3), part(9)], axis=1)
    ws = jnp.pad(small, ((0, 0), (0, 128 - small.shape[1]))).astype(BF16)
    wa = jnp.pad(w_gla_a, ((0, 128 - GLA_RANK), (0, 0))).astype(BF16)
    ba = b_gla_a.reshape(1, GQ_W).astype(F32)
    bg = jnp.pad(b_nsa_gate.reshape(1, -1), ((0, 0), (GATE_OFF, 128 - GATE_OFF - 3 * NSA_HEADS))).astype(F32)
    return wm, ws, wa, ba, bg


def kernel(x_prompt, x_sample, cache_kv_cmp, cache_kv_sel, cache_kv_win, state_gla, state_conv, page_table,
           norm_mix, w_in, w_gla_a, b_gla_a, gla_norm, b_nsa_gate, w_out, norm_ffn, w_ffn_up, conv_w, conv_b,
           w_ffn_down, norm_final):
    bp, seq, d = x_prompt.shape
    db, dseq, _ = x_sample.shape
    assert dseq == 1 and norm_mix.shape[0] == 1
    npg = page_table.shape[1]
    psz = cache_kv_cmp.shape[2]
    past = npg * psz
    n_phys = cache_kv_cmp.shape[1]
    w_buf = cache_kv_win.shape[2]
    dff = w_ffn_down.shape[1]
    assert past % L_SEL == 0 and w_buf == WINDOW and past >= WINDOW and seq >= WINDOW + Q_BLOCK
    tm = min(512, seq)

    wm, ws, wa, ba, bg = _pack_in_weights(w_in[0], w_gla_a[0], b_gla_a[0], b_nsa_gate[0])
    nmix = norm_mix[0].reshape(1, d).astype(F32)
    wo = w_out[0].astype(BF16)
    nf = norm_ffn[0].reshape(1, d).astype(F32)
    wup = w_ffn_up[0].astype(BF16)
    cw = conv_w[0].astype(F32)
    cb = conv_b[0].reshape(1, dff).astype(F32)
    wdn = w_ffn_down[0].astype(BF16)
    nl = norm_final.reshape(1, d).astype(F32)

    (gq, gk, gv, gr, glog, kvc, kvs, kvw, qa, ksa, kwa, vst, vwt, cm, gates_t) = _proj(
        x_prompt, nmix, wm, ws, wa, ba, bg, tm, True)
    y_gla, s_p = _gla_prompt(gq, gk, gv, glog, gr, gla_norm[0], bp, seq, tm)
    cm_perm = jnp.concatenate([cm[:, 0::2], cm[:, 1::2]], axis=1)
    oc, selm, anyf = _nsa_cmp(qa, cm_perm)
    bits = _tile_bits(anyf, seq // L_SEL)
    y_nsa = _nsa_sel(bits, qa, ksa, kwa, vst, vwt, selm, oc, gates_t)
    y_p, conv_p = _ffn(x_prompt.reshape(bp * seq, d), y_gla, y_nsa, wo, nf, wup, cw, cb, wdn, nl, None, bp, seq, tm)
    w_keep = min(WINDOW, seq)

    def kv_rows(a_t):
        n_b, _, n_t = a_t.shape
        return a_t.reshape(1, n_b, 2, NSA_G, NSA_DH, n_t).transpose(0, 1, 5, 2, 3, 4)

    def token_minor(cache):
        return cache.transpose(0, 2, 3, 4, 1)

    out_prompt = (y_p.reshape(bp, seq, d), kv_rows(kvc), kv_rows(kvs), kv_rows(kvw[:, :, seq - w_keep:]),
                  s_p[None], conv_p[None])

    xs3 = x_sample.reshape(1, db, d)
    (gq, gk, gv, gr, glog, kvc_st, kvs_st, kvw_st, nqh, gates, kvs_s, kvw_s) = _proj(
        xs3, nmix, wm, ws, wa, ba, bg, db, False)
    y_gla_s, s_s = _gla_step(gq, gk, glog, gv, gr, gla_norm[0], state_gla[0].astype(F32))
    q3 = nqh[0].transpose(1, 0, 2)
    cm_s = _page_means(token_minor(cache_kv_cmp[0]), page_table)
    k_sel = min(N_SEL, past // L_SEL + 1) - 1
    oc_s, idx = _nsa_cmp_step(q3, cm_s, past, k_sel, bb=min(8, db))
    yn_s, win_out = _nsa_sel_step(page_table, idx, q3, oc_s, gates.reshape(db, 1, 128), kvs_s.reshape(db, 1, KV_W),
                                  kvw_s.reshape(db, 1, KV_W), kvw_st[0],
                                  token_minor(cache_kv_win[0]).reshape(db, KV_W, w_buf),
                                  token_minor(cache_kv_sel[0]), past, k_sel)
    y_s, conv_s = _ffn(x_sample.reshape(db, d), y_gla_s, yn_s.reshape(db, NQ_W), wo, nf, wup, cw, cb, wdn, nl,
                       state_conv[0].reshape(db, (CONV_W - 1) * dff).astype(F32), db, 1, db)
    sample_rows = lambda a_t: a_t[0].T.reshape(1, db, 1, 2, NSA_G, NSA_DH)
    out_sample = (y_s.reshape(db, 1, d), sample_rows(kvc_st), sample_rows(kvs_st), kv_rows(win_out),
                  s_s.reshape(1, db, GLA_HEADS, GLA_DK, GLA_DV), conv_s.reshape(1, db, CONV_W - 1, dff))
    return (out_prompt[0], out_sample[0]) + out_prompt[1:] + out_sample[1:]
```
